```python
import math
import jax
import jax.numpy as jnp
from jax import lax
import numpy as np

D_MODEL = 2048
BATCH = 4
SEQ = 4096
DEPTH = 2
DEC_BATCH = 2
DEC_SEQ = 4096
PAST_LEN = 128

N_MIXERS = 2
N_ATTN = (DEPTH + N_MIXERS - 1) // N_MIXERS
N_CONV = DEPTH // N_MIXERS
HEAD_DIM = 64
N_HEADS = D_MODEL // (2 * HEAD_DIM)
Q_BLOCK = 128
CONV_WIDTH = 31
CONV_PAD = (CONV_WIDTH - 1) // 2
D_FF = 4 * D_MODEL
RMS_EPS = 1e-6
SUBLN_EPS = 1e-5
LN_EPS = 1e-5

kernel_name = "hybrid_diffattn_conformer_encoder"


def rmsnorm(x, g, eps=RMS_EPS):
    xf = x.astype(jnp.float32)
    xf = xf * lax.rsqrt(jnp.mean(xf * xf, axis=-1, keepdims=True) + eps)
    return (xf * g.astype(jnp.float32)).astype(x.dtype)


def layernorm(x, g, b, eps=LN_EPS):
    xf = x.astype(jnp.float32)
    mu = jnp.mean(xf, axis=-1, keepdims=True)
    var = jnp.mean(jnp.square(xf - mu), axis=-1, keepdims=True)
    y = (xf - mu) * lax.rsqrt(var + eps) * g.astype(jnp.float32) + b.astype(jnp.float32)
    return y.astype(x.dtype)


def alibi_slopes(n):
    return jnp.exp2(-8.0 * jnp.arange(1, n + 1, dtype=jnp.float32) / n)


def lambda_init_for(layer_idx):
    return 0.8 - 0.6 * math.exp(-0.3 * layer_idx)


def diff_attention(h, w_qkv, lam_q1, lam_k1, lam_q2, lam_k2, subln_g, w_o, lambda_init):
    B, S, _ = h.shape
    n_blk = S // Q_BLOCK
    q, k, v = jnp.split(h @ w_qkv, 3, axis=-1)
    q = q.reshape(B, n_blk, Q_BLOCK, 2, N_HEADS, HEAD_DIM).transpose(1, 0, 2, 3, 4, 5)
    k = k.reshape(B, S, 2, N_HEADS, HEAD_DIM)
    v = v.reshape(B, S, N_HEADS, 2 * HEAD_DIM)
    lam = (jnp.exp(jnp.sum(lam_q1.astype(jnp.float32) * lam_k1.astype(jnp.float32)))
           - jnp.exp(jnp.sum(lam_q2.astype(jnp.float32) * lam_k2.astype(jnp.float32)))
           + lambda_init)
    slopes = alibi_slopes(N_HEADS)
    k_pos = jnp.arange(S, dtype=jnp.int32)
    scale = HEAD_DIM ** -0.5

    def block(args):
        q_blk, blk = args
        q_pos = blk * Q_BLOCK + jnp.arange(Q_BLOCK, dtype=jnp.int32)
        dist = jnp.abs(q_pos[:, None] - k_pos[None, :]).astype(jnp.float32)
        bias = -slopes[:, None, None] * dist[None]
        s = jnp.einsum("bqmhd,bkmhd->bmhqk", q_blk, k).astype(jnp.float32) * scale + bias
        p = jax.nn.softmax(s, axis=-1)
        a = p[:, 0] - lam * p[:, 1]
        return jnp.einsum("bhqk,bkhe->bqhe", a.astype(v.dtype), v)

    o = lax.map(block, (q, jnp.arange(n_blk, dtype=jnp.int32)))
    o = o.transpose(1, 0, 2, 3, 4).reshape(B, S, N_HEADS, 2 * HEAD_DIM)
    o = rmsnorm(o, subln_g, SUBLN_EPS) * (1.0 - lambda_init)
    return o.reshape(B, S, D_MODEL) @ w_o


def conformer_conv(h, w_pw1, b_pw1, w_dw, b_dw, ln_g, ln_b, w_pw2, b_pw2):
    u = h @ w_pw1 + b_pw1
    a, g = jnp.split(u, 2, axis=-1)
    u = a * jax.nn.sigmoid(g)
    u = lax.conv_general_dilated(
        u, w_dw[:, None, :], window_strides=(1,), padding=[(CONV_PAD, CONV_PAD)],
        dimension_numbers=('NWC', 'WIO', 'NWC'), feature_group_count=D_MODEL) + b_dw
    u = jax.nn.silu(layernorm(u, ln_g, ln_b))
    return u @ w_pw2 + b_pw2


def sq_relu_mlp(h, w_up, w_down):
    return jnp.square(jax.nn.relu(h @ w_up)) @ w_down


def encoder_trunk(x, attn_norm_g, w_qkv, lam_q1, lam_k1, lam_q2, lam_k2, subln_g, w_o,
                  conv_norm_g, conv_w_pw1, conv_b_pw1, conv_w_dw, conv_b_dw, conv_ln_g,
                  conv_ln_b, conv_w_pw2, conv_b_pw2, mlp_norm_g, w_up, w_down, final_norm_g):
    for i in range(DEPTH):
        j = i // N_MIXERS
        if i % N_MIXERS == 0:
            x = x + diff_attention(rmsnorm(x, attn_norm_g[j]), w_qkv[j], lam_q1[j], lam_k1[j],
                                   lam_q2[j], lam_k2[j], subln_g[j], w_o[j], lambda_init_for(i))
        else:
            x = x + conformer_conv(rmsnorm(x, conv_norm_g[j]), conv_w_pw1[j], conv_b_pw1[j],
                                   conv_w_dw[j], conv_b_dw[j], conv_ln_g[j], conv_ln_b[j],
                                   conv_w_pw2[j], conv_b_pw2[j])
        x = x + sq_relu_mlp(rmsnorm(x, mlp_norm_g[i]), w_up[i], w_down[i])
    return rmsnorm(x, final_norm_g)


def setup_inputs(seed: int = 0) -> dict:
    key = jax.random.key(seed)
    ks = jax.random.split(key, 24)
    f32 = jnp.float32
    D = D_MODEL

    def nrm(k, shape, scale):
        return jax.random.normal(k, shape, dtype=f32) * scale

    def gain(k, shape):
        return 1.0 + 0.01 * jax.random.normal(k, shape, dtype=f32)

    return {
        "x_prompt": nrm(ks[0], (BATCH, SEQ, D), 1.0),
        "x_sample": nrm(ks[1], (DEC_BATCH, DEC_SEQ, D), 1.0),
        "attn_norm_g": gain(ks[2], (N_ATTN, D)),
        "w_qkv": nrm(ks[3], (N_ATTN, D, 3 * D), D ** -0.5),
        "lam_q1": nrm(ks[4], (N_ATTN, HEAD_DIM), 0.1),
        "lam_k1": nrm(ks[5], (N_ATTN, HEAD_DIM), 0.1),
        "lam_q2": nrm(ks[6], (N_ATTN, HEAD_DIM), 0.1),
        "lam_k2": nrm(ks[7], (N_ATTN, HEAD_DIM), 0.1),
        "subln_g": gain(ks[8], (N_ATTN, 2 * HEAD_DIM)),
        "w_o": nrm(ks[9], (N_ATTN, D, D), D ** -0.5),
        "conv_norm_g": gain(ks[10], (N_CONV, D)),
        "conv_w_pw1": nrm(ks[11], (N_CONV, D, 2 * D), D ** -0.5),
        "conv_b_pw1": nrm(ks[12], (N_CONV, 2 * D), 0.01),
        "conv_w_dw": nrm(ks[13], (N_CONV, CONV_WIDTH, D), CONV_WIDTH ** -0.5),
        "conv_b_dw": nrm(ks[14], (N_CONV, D), 0.01),
        "conv_ln_g": gain(ks[15], (N_CONV, D)),
        "conv_ln_b": nrm(ks[16], (N_CONV, D), 0.01),
        "conv_w_pw2": nrm(ks[17], (N_CONV, D, D), D ** -0.5),
        "conv_b_pw2": nrm(ks[18], (N_CONV, D), 0.01),
        "mlp_norm_g": gain(ks[19], (DEPTH, D)),
        "w_up": nrm(ks[20], (DEPTH, D, D_FF), D ** -0.5),
        "w_down": nrm(ks[21], (DEPTH, D_FF, D), D_FF ** -0.5),
        "final_norm_g": gain(ks[22], (D,)),
    }


def reference(x_prompt, x_sample, attn_norm_g, w_qkv, lam_q1, lam_k1, lam_q2, lam_k2, subln_g,
              w_o, conv_norm_g, conv_w_pw1, conv_b_pw1, conv_w_dw, conv_b_dw, conv_ln_g,
              conv_ln_b, conv_w_pw2, conv_b_pw2, mlp_norm_g, w_up, w_down, final_norm_g):
    y_prompt = encoder_trunk(x_prompt, attn_norm_g, w_qkv, lam_q1, lam_k1, lam_q2, lam_k2,
                             subln_g, w_o, conv_norm_g, conv_w_pw1, conv_b_pw1, conv_w_dw,
                             conv_b_dw, conv_ln_g, conv_ln_b, conv_w_pw2, conv_b_pw2,
                             mlp_norm_g, w_up, w_down, final_norm_g)
    y_sample = encoder_trunk(x_sample, attn_norm_g, w_qkv, lam_q1, lam_k1, lam_q2, lam_k2,
                             subln_g, w_o, conv_norm_g, conv_w_pw1, conv_b_pw1, conv_w_dw,
                             conv_b_dw, conv_ln_g, conv_ln_b, conv_w_pw2, conv_b_pw2,
                             mlp_norm_g, w_up, w_down, final_norm_g)
    return (y_prompt, y_sample)
```

```python
import functools
import math

import jax
import jax.numpy as jnp
from jax import lax
from jax.experimental import pallas as pl
from jax.experimental.pallas import tpu as pltpu

D_MODEL = 2048
SEQ = 4096
HEAD_DIM = 64
N_HEADS = D_MODEL // (2 * HEAD_DIM)
V_DIM = 2 * HEAD_DIM
CONV_WIDTH = 31
CONV_PAD = (CONV_WIDTH - 1) // 2
D_FF = 4 * D_MODEL
RMS_EPS = 1e-6
SUBLN_EPS = 1e-5
LN_EPS = 1e-5

LANES = 128
VMEM_LIMIT = 56 * 1024 * 1024

F32 = jnp.float32
BF16 = jnp.bfloat16


def _params(semantics):
    return pltpu.CompilerParams(dimension_semantics=semantics, vmem_limit_bytes=VMEM_LIMIT)


def _rmsnorm_rows(x, g, eps):
    return x * lax.rsqrt(jnp.mean(x * x, axis=-1, keepdims=True) + eps) * g


def _norm_matmul_kernel(x_ref, g_ref, w_ref, o_ref, xn_ref):
    @pl.when(pl.program_id(1) == 0)
    def _():
        xn_ref[...] = _rmsnorm_rows(x_ref[...], g_ref[...], RMS_EPS).astype(BF16)

    o_ref[...] = jnp.dot(xn_ref[...], w_ref[...], preferred_element_type=F32).astype(o_ref.dtype)


def norm_matmul(x, g, w, *, tm, tn):
    T, D = x.shape
    N = w.shape[1]
    return pl.pallas_call(
        _norm_matmul_kernel,
        grid=(T // tm, N // tn),
        in_specs=[
            pl.BlockSpec((tm, D), lambda i, j: (i, 0)),
            pl.BlockSpec((1, D), lambda i, j: (0, 0)),
            pl.BlockSpec((D, tn), lambda i, j: (0, j)),
        ],
        out_specs=pl.BlockSpec((tm, tn), lambda i, j: (i, j)),
        out_shape=jax.ShapeDtypeStruct((T, N), BF16),
        scratch_shapes=[pltpu.VMEM((tm, D), BF16)],
        compiler_params=_params(("parallel", "arbitrary")),
        name="qkv",
    )(x, g.reshape(1, D), w)


def _attn_kernel(lam_ref, g_ref, q_ref, k_ref, v_ref, o_ref,
                 bias_ref, vaug_ref, s_ref, m_ref, acc_ref, *, tq, tk, lambda_init):
    S = k_ref.shape[0]
    h = pl.program_id(1)
    qi = pl.program_id(2)
    n_kc = S // tk

    @pl.when(qi == 0)
    def _():
        slope = jnp.exp2(jnp.full((1, 1), -0.5, F32) * (h + 1).astype(F32))
        r = lax.broadcasted_iota(jnp.int32, (tq, 2 * S), 0)
        j = lax.broadcasted_iota(jnp.int32, (tq, 2 * S), 1)
        bias_ref[...] = jnp.abs(r - j + S).astype(F32) * (-slope)
        lane = lax.broadcasted_iota(jnp.int32, (S, LANES), 1)
        vaug_ref[:, :V_DIM] = v_ref[...]
        vaug_ref[:, V_DIM:] = jnp.where(lane == 0, 1.0, 0.0).astype(BF16)

    q = q_ref[...]
    lane = lax.broadcasted_iota(jnp.int32, q.shape, 1)
    scale = jnp.asarray(HEAD_DIM ** -0.5, BF16)
    zero = jnp.zeros_like(q)
    qq = jnp.concatenate([jnp.where(lane < HEAD_DIM, q, zero),
                          jnp.where(lane >= HEAD_DIM, q, zero)], axis=0) * scale

    b0 = pl.multiple_of(S - qi * tq, LANES)

    def pass1(c, m_run):
        c0 = pl.multiple_of(c * tk, tk)
        kc = k_ref[pl.ds(c0, tk), :]
        s = lax.dot_general(qq, kc, (((1,), (1,)), ((), ())), preferred_element_type=F32)
        bias = bias_ref[:, pl.ds(pl.multiple_of(b0 + c0, LANES), tk)]
        s = s + jnp.concatenate([bias, bias], axis=0)
        s_ref[:, pl.ds(c0, tk)] = s
        return jnp.maximum(m_run, jnp.max(s, axis=-1, keepdims=True))

    m = lax.fori_loop(0, n_kc, pass1, jnp.full((2 * tq, 1), -jnp.inf, F32))
    m_ref[...] = jnp.broadcast_to(m, m_ref.shape)
    acc_ref[...] = jnp.zeros_like(acc_ref)

    def pass2(c, carry):
        c0 = pl.multiple_of(c * tk, tk)
        mb = m_ref[...]
        s = s_ref[:, pl.ds(c0, tk)]
        p = jnp.exp(s - jnp.tile(mb, (1, tk // LANES))).astype(BF16)
        acc_ref[...] += jnp.dot(p, vaug_ref[pl.ds(c0, tk), :], preferred_element_type=F32)
        return carry

    lax.fori_loop(0, n_kc, pass2, 0)

    lv = lam_ref[...]
    lam = (jnp.exp(jnp.sum(lv[0:1] * lv[1:2], axis=-1, keepdims=True))
           - jnp.exp(jnp.sum(lv[2:3] * lv[3:4], axis=-1, keepdims=True)) + lambda_init)

    acc = acc_ref[...]
    o0 = acc[:tq, :V_DIM] / acc[:tq, V_DIM:V_DIM + 1]
    o1 = acc[tq:, :V_DIM] / acc[tq:, V_DIM:V_DIM + 1]
    o = o0 - lam * o1
    o = _rmsnorm_rows(o, g_ref[...], SUBLN_EPS) * (1.0 - lambda_init)
    o_ref[...] = o.astype(o_ref.dtype)


def diff_attention(qkv, lam_vecs, subln_g, lambda_init, *, n_seq, tq=256, tk=512):
    T = qkv.shape[0]
    S = T // n_seq
    nq = S // tq
    H = N_HEADS
    kern = functools.partial(_attn_kernel, tq=tq, tk=tk, lambda_init=lambda_init)
    return pl.pallas_call(
        kern,
        grid=(n_seq, H, nq),
        in_specs=[
            pl.BlockSpec((4, HEAD_DIM), lambda b, h, i: (0, 0)),
            pl.BlockSpec((1, V_DIM), lambda b, h, i: (0, 0)),
            pl.BlockSpec((tq, LANES), lambda b, h, i: (b * nq + i, h)),
            pl.BlockSpec((S, LANES), lambda b, h, i: (b, H + h)),
            pl.BlockSpec((S, LANES), lambda b, h, i: (b, 2 * H + h)),
        ],
        out_specs=pl.BlockSpec((tq, V_DIM), lambda b, h, i: (b * nq + i, h)),
        out_shape=jax.ShapeDtypeStruct((T, D_MODEL), BF16),
        scratch_shapes=[
            pltpu.VMEM((tq, 2 * S), F32),
            pltpu.VMEM((S, 2 * LANES), BF16),
            pltpu.VMEM((2 * tq, S), F32),
            pltpu.VMEM((2 * tq, LANES), F32),
            pltpu.VMEM((2 * tq, 2 * LANES), F32),
        ],
        compiler_params=_params(("parallel", "arbitrary", "arbitrary")),
        name="attn",
    )(lam_vecs, subln_g.reshape(1, V_DIM), qkv, qkv, qkv)


def _proj_kernel(x_ref, a_ref, w_ref, o_ref):
    o_ref[...] = x_ref[...] + jnp.dot(a_ref[...], w_ref[...], preferred_element_type=F32)


def proj_residual(x, a, w, *, tm):
    T, D = x.shape
    return pl.pallas_call(
        _proj_kernel,
        grid=(T // tm,),
        in_specs=[
            pl.BlockSpec((tm, D), lambda i: (i, 0)),
            pl.BlockSpec((tm, D), lambda i: (i, 0)),
            pl.BlockSpec((D, D), lambda i: (0, 0)),
        ],
        out_specs=pl.BlockSpec((tm, D), lambda i: (i, 0)),
        out_shape=jax.ShapeDtypeStruct((T, D), F32),
        compiler_params=_params(("parallel",)),
        name="proj",
    )(x, a, w)


def _mlp_kernel(x_ref, g_ref, wu_ref, wd_ref, gf_ref, o_ref, xn_ref, *, final_norm):
    j = pl.program_id(1)

    @pl.when(j == 0)
    def _():
        x = x_ref[...]
        xn_ref[...] = _rmsnorm_rows(x, g_ref[...], RMS_EPS).astype(BF16)
        o_ref[...] = x

    hdn = jnp.dot(xn_ref[...], wu_ref[...], preferred_element_type=F32)
    hdn = jnp.maximum(hdn, 0.0)
    hdn = (hdn * hdn).astype(BF16)
    o_ref[...] += jnp.dot(hdn, wd_ref[...], preferred_element_type=F32)

    if final_norm:
        @pl.when(j == pl.num_programs(1) - 1)
        def _():
            o_ref[...] = _rmsnorm_rows(o_ref[...], gf_ref[...], RMS_EPS)


def mlp_residual(x, g, w_up, w_down, g_final, *, final_norm, tm, tf):
    T, D = x.shape
    F = w_up.shape[1]
    kern = functools.partial(_mlp_kernel, final_norm=final_norm)
    return pl.pallas_call(
        kern,
        grid=(T // tm, F // tf),
        in_specs=[
            pl.BlockSpec((tm, D), lambda i, j: (i, 0)),
            pl.BlockSpec((1, D), lambda i, j: (0, 0)),
            pl.BlockSpec((D, tf), lambda i, j: (0, j)),
            pl.BlockSpec((tf, D), lambda i, j: (j, 0)),
            pl.BlockSpec((1, D), lambda i, j: (0, 0)),
        ],
        out_specs=pl.BlockSpec((tm, D), lambda i, j: (i, 0)),
        out_shape=jax.ShapeDtypeStruct((T, D), F32),
        scratch_shapes=[pltpu.VMEM((tm, D), BF16)],
        compiler_params=_params(("parallel", "arbitrary")),
        name="mlp",
    )(x, g.reshape(1, D), w_up, w_down, g_final.reshape(1, D))


def _glu_kernel(x_ref, g_ref, wa_ref, wg_ref, ba_ref, bg_ref, o_ref, xn_ref):
    @pl.when(pl.program_id(1) == 0)
    def _():
        xn_ref[...] = _rmsnorm_rows(x_ref[...], g_ref[...], RMS_EPS).astype(BF16)

    xn = xn_ref[...]
    a = jnp.dot(xn, wa_ref[...], preferred_element_type=F32) + ba_ref[...]
    gt = jnp.dot(xn, wg_ref[...], preferred_element_type=F32) + bg_ref[...]
    o_ref[...] = a * jax.nn.sigmoid(gt)


def norm_glu(x, g, w_pw1, b_pw1, *, tm, tn):
    T, D = x.shape
    nj = D // tn
    b2 = b_pw1.reshape(1, 2 * D)
    return pl.pallas_call(
        _glu_kernel,
        grid=(T // tm, nj),
        in_specs=[
            pl.BlockSpec((tm, D), lambda i, j: (i, 0)),
            pl.BlockSpec((1, D), lambda i, j: (0, 0)),
            pl.BlockSpec((D, tn), lambda i, j: (0, j)),
            pl.BlockSpec((D, tn), lambda i, j: (0, nj + j)),
            pl.BlockSpec((1, tn), lambda i, j: (0, j)),
            pl.BlockSpec((1, tn), lambda i, j: (0, nj + j)),
        ],
        out_specs=pl.BlockSpec((tm, tn), lambda i, j: (i, j)),
        out_shape=jax.ShapeDtypeStruct((T, D), F32),
        scratch_shapes=[pltpu.VMEM((tm, D), BF16)],
        compiler_params=_params(("parallel", "arbitrary")),
        name="glu",
    )(x, g.reshape(1, D), w_pw1, w_pw1, b2, b2)


HALO = 16


def _conv_kernel(x_ref, u_ref, up_ref, un_ref, wdw_ref, bdw_ref, lg_ref, lb_ref, w2_ref, b2_ref,
                 o_ref, pad_ref, y_ref, *, tm, tc, tiles_per_seq):
    i = pl.program_id(0)
    first = (i % tiles_per_seq) == 0
    last = (i % tiles_per_seq) == tiles_per_seq - 1
    pad_ref[0:HALO, :] = jnp.where(first, 0.0, up_ref[...])
    pad_ref[HALO:HALO + tm, :] = u_ref[...]
    pad_ref[HALO + tm:, :] = jnp.where(last, 0.0, un_ref[...])

    D = u_ref.shape[1]
    off = HALO - CONV_PAD

    def cols(c, carry):
        c0 = pl.multiple_of(c * tc, tc)
        w = wdw_ref[:, pl.ds(c0, tc)]
        acc = jnp.zeros((tm, tc), F32)
        for t in range(CONV_WIDTH):
            acc = acc + pad_ref[off + t:off + t + tm, pl.ds(c0, tc)] * w[t:t + 1, :]
        y_ref[:, pl.ds(c0, tc)] = acc
        return carry

    lax.fori_loop(0, D // tc, cols, 0)

    y = y_ref[...] + bdw_ref[...]
    mu = jnp.mean(y, axis=-1, keepdims=True)
    yc = y - mu
    var = jnp.mean(yc * yc, axis=-1, keepdims=True)
    z = yc * lax.rsqrt(var + LN_EPS) * lg_ref[...] + lb_ref[...]
    z = (z * jax.nn.sigmoid(z)).astype(BF16)
    o_ref[...] = x_ref[...] + jnp.dot(z, w2_ref[...], preferred_element_type=F32) + b2_ref[...]


def conv_residual(x, u, w_dw, b_dw, ln_g, ln_b, w_pw2, b_pw2, *, n_seq, tm=512, tc=256):
    T, D = x.shape
    S = T // n_seq
    tiles_per_seq = S // tm
    hb = tm // HALO
    n_hb = T // HALO
    kern = functools.partial(_conv_kernel, tm=tm, tc=tc, tiles_per_seq=tiles_per_seq)
    row = lambda v: v.reshape(1, D)
    return pl.pallas_call(
        kern,
        grid=(T // tm,),
        in_specs=[
            pl.BlockSpec((tm, D), lambda i: (i, 0)),
            pl.BlockSpec((tm, D), lambda i: (i, 0)),
            pl.BlockSpec((HALO, D), lambda i: (jnp.maximum(i * hb - 1, 0), 0)),
            pl.BlockSpec((HALO, D), lambda i: (jnp.minimum((i + 1) * hb, n_hb - 1), 0)),
            pl.BlockSpec((CONV_WIDTH, D), lambda i: (0, 0)),
            pl.BlockSpec((1, D), lambda i: (0, 0)),
            pl.BlockSpec((1, D), lambda i: (0, 0)),
            pl.BlockSpec((1, D), lambda i: (0, 0)),
            pl.BlockSpec((D, D), lambda i: (0, 0)),
            pl.BlockSpec((1, D), lambda i: (0, 0)),
        ],
        out_specs=pl.BlockSpec((tm, D), lambda i: (i, 0)),
        out_shape=jax.ShapeDtypeStruct((T, D), F32),
        scratch_shapes=[pltpu.VMEM((tm + 2 * HALO, D), F32), pltpu.VMEM((tm, D), F32)],
        compiler_params=_params(("parallel",)),
        name="conv",
    )(x, u, u, u, w_dw, row(b_dw), row(ln_g), row(ln_b), w_pw2, row(b_pw2))


def _lambda_init_for(layer_idx):
    return 0.8 - 0.6 * math.exp(-0.3 * layer_idx)


def _head_major(w):
    D = w.shape[0]
    return w.reshape(D, 2, N_HEADS, HEAD_DIM).transpose(0, 2, 1, 3).reshape(D, 2 * N_HEADS * HEAD_DIM)


def kernel(x_prompt, x_sample, attn_norm_g, w_qkv, lam_q1, lam_k1, lam_q2, lam_k2, subln_g, w_o,
           conv_norm_g, conv_w_pw1, conv_b_pw1, conv_w_dw, conv_b_dw, conv_ln_g, conv_ln_b,
           conv_w_pw2, conv_b_pw2, mlp_norm_g, w_up, w_down, final_norm_g):
    D = D_MODEL
    n_prompt = x_prompt.shape[0]
    n_seq = n_prompt + x_sample.shape[0]
    x = jnp.concatenate([x_prompt.reshape(-1, D), x_sample.reshape(-1, D)], axis=0)

    wq, wk, wv = jnp.split(w_qkv[0], 3, axis=-1)
    w_qkv_b = jnp.concatenate([_head_major(wq), _head_major(wk), wv], axis=-1).astype(BF16)
    lam_vecs = jnp.stack([lam_q1[0], lam_k1[0], lam_q2[0], lam_k2[0]]).astype(F32)
    qkv = norm_matmul(x, attn_norm_g[0], w_qkv_b, tm=1024, tn=1024)
    o = diff_attention(qkv, lam_vecs, subln_g[0], _lambda_init_for(0), n_seq=n_seq)
    x = proj_residual(x, o, w_o[0].astype(BF16), tm=1024)
    x = mlp_residual(x, mlp_norm_g[0], w_up[0].astype(BF16), w_down[0].astype(BF16), final_norm_g,
                     final_norm=False, tm=1024, tf=512)

    u = norm_glu(x, conv_norm_g[0], conv_w_pw1[0].astype(BF16), conv_b_pw1[0], tm=1024, tn=512)
    x = conv_residual(x, u, conv_w_dw[0], conv_b_dw[0], conv_ln_g[0], conv_ln_b[0],
                      conv_w_pw2[0].astype(BF16), conv_b_pw2[0], n_seq=n_seq)
    x = mlp_residual(x, mlp_norm_g[1], w_up[1].astype(BF16), w_down[1].astype(BF16), final_norm_g,
                     final_norm=True, tm=1024, tf=512)

    y_prompt = x[:n_prompt * SEQ].reshape(x_prompt.shape)
    y_sample = x[n_prompt * SEQ:].reshape(x_sample.shape)
    return (y_prompt, y_sample)
```

```python
import functools
import math

import jax
import jax.numpy as jnp
from jax import lax
from jax.experimental import pallas as pl
from jax.experimental.pallas import tpu as pltpu

D_MODEL = 2048
SEQ = 4096
HEAD_DIM = 64
N_HEADS = D_MODEL // (2 * HEAD_DIM)
V_DIM = 2 * HEAD_DIM
CONV_WIDTH = 31
CONV_PAD = (CONV_WIDTH - 1) // 2
D_FF = 4 * D_MODEL
RMS_EPS = 1e-6
SUBLN_EPS = 1e-5
LN_EPS = 1e-5

LANES = 128
VMEM_LIMIT = 56 * 1024 * 1024

F32 = jnp.float32
BF16 = jnp.bfloat16


def _params(semantics):
    return pltpu.CompilerParams(dimension_semantics=semantics, vmem_limit_bytes=VMEM_LIMIT)


def _rmsnorm_rows(x, g, eps):
    return x * lax.rsqrt(jnp.mean(x * x, axis=-1, keepdims=True) + eps) * g


def _norm_matmul_kernel(x_ref, g_ref, w_ref, o_ref, xn_ref):
    @pl.when(pl.program_id(1) == 0)
    def _():
        xn_ref[...] = _rmsnorm_rows(x_ref[...], g_ref[...], RMS_EPS).astype(BF16)

    o_ref[...] = jnp.dot(xn_ref[...], w_ref[...], preferred_element_type=F32).astype(o_ref.dtype)


def norm_matmul(x, g, w, *, tm, tn):
    T, D = x.shape
    N = w.shape[1]
    return pl.pallas_call(
        _norm_matmul_kernel,
        grid=(T // tm, N // tn),
        in_specs=[
            pl.BlockSpec((tm, D), lambda i, j: (i, 0)),
            pl.BlockSpec((1, D), lambda i, j: (0, 0)),
            pl.BlockSpec((D, tn), lambda i, j: (0, j)),
        ],
        out_specs=pl.BlockSpec((tm, tn), lambda i, j: (i, j)),
        out_shape=jax.ShapeDtypeStruct((T, N), BF16),
        scratch_shapes=[pltpu.VMEM((tm, D), BF16)],
        compiler_params=_params(("parallel", "arbitrary")),
        name="qkv",
    )(x, g.reshape(1, D), w)


def _attn_kernel(lam_ref, g_ref, q_ref, k_ref, v_ref, o_ref,
                 bias_ref, vaug_ref, s0_ref, s1_ref, m0_ref, m1_ref,
                 *, tq, nq, n_tiles, lambda_init):
    S = k_ref.shape[0]
    g = pl.program_id(0)
    ga = jnp.minimum(g, n_tiles - 1)
    gb = jnp.maximum(g - 1, 0)
    qi = ga % nq
    h = (ga // nq) % N_HEADS

    @pl.when(g == 0)
    def _():
        s1_ref[...] = jnp.zeros_like(s1_ref)
        m1_ref[...] = jnp.zeros_like(m1_ref)

    @pl.when(qi == 0)
    def _():
        slope = jnp.exp2(jnp.full((1, 1), -0.5, F32) * (h + 1).astype(F32))
        r = lax.broadcasted_iota(jnp.int32, (tq, 2 * S), 0)
        j = lax.broadcasted_iota(jnp.int32, (tq, 2 * S), 1)
        bias_ref[...] = jnp.abs(r - j + S).astype(F32) * (-slope)

    @pl.when(gb % nq == 0)
    def _():
        lane = lax.broadcasted_iota(jnp.int32, (S, LANES), 1)
        vaug_ref[:, :V_DIM] = v_ref[...]
        vaug_ref[:, V_DIM:] = jnp.where(lane == 0, 1.0, 0.0).astype(BF16)

    lv = lam_ref[...]
    lam = (jnp.exp(jnp.sum(lv[0:1] * lv[1:2], axis=-1, keepdims=True))
           - jnp.exp(jnp.sum(lv[2:3] * lv[3:4], axis=-1, keepdims=True)) + lambda_init)
    b0 = pl.multiple_of(S - qi * tq, LANES)

    def step(s_w, m_w, s_r, m_r):
        q = q_ref[...]
        lane = lax.broadcasted_iota(jnp.int32, q.shape, 1)
        scale = jnp.asarray(HEAD_DIM ** -0.5, BF16)
        zero = jnp.zeros_like(q)
        qq = jnp.concatenate([jnp.where(lane < HEAD_DIM, q, zero),
                              jnp.where(lane >= HEAD_DIM, q, zero)], axis=0) * scale
        s = lax.dot_general(qq, k_ref[...], (((1,), (1,)), ((), ())), preferred_element_type=F32)
        bias = bias_ref[:, pl.ds(b0, S)]
        s = s + jnp.concatenate([bias, bias], axis=0)
        s_w[...] = s
        m_w[...] = jnp.broadcast_to(jnp.max(s, axis=-1, keepdims=True), m_w.shape)

        p = jnp.exp(s_r[...] - jnp.tile(m_r[...], (1, S // LANES))).astype(BF16)
        acc = jnp.dot(p, vaug_ref[...], preferred_element_type=F32)
        o0 = acc[:tq, :V_DIM] / acc[:tq, V_DIM:V_DIM + 1]
        o1 = acc[tq:, :V_DIM] / acc[tq:, V_DIM:V_DIM + 1]
        o = o0 - lam * o1
        o = _rmsnorm_rows(o, g_ref[...], SUBLN_EPS) * (1.0 - lambda_init)
        o_ref[...] = o.astype(o_ref.dtype)

    @pl.when(g % 2 == 0)
    def _():
        step(s0_ref, m0_ref, s1_ref, m1_ref)

    @pl.when(g % 2 == 1)
    def _():
        step(s1_ref, m1_ref, s0_ref, m0_ref)


def diff_attention(qkv, lam_vecs, subln_g, lambda_init, *, n_seq, tq=256):
    T = qkv.shape[0]
    S = T // n_seq
    nq = S // tq
    H = N_HEADS
    n_tiles = n_seq * H * nq
    kern = functools.partial(_attn_kernel, tq=tq, nq=nq, n_tiles=n_tiles, lambda_init=lambda_init)

    def tile(t):
        return t // (nq * H), (t // nq) % H, t % nq

    def q_map(g):
        b, h, i = tile(jnp.minimum(g, n_tiles - 1))
        return (b * nq + i, h)

    def k_map(g):
        b, h, _ = tile(jnp.minimum(g, n_tiles - 1))
        return (b, H + h)

    def v_map(g):
        b, h, _ = tile(jnp.maximum(g - 1, 0))
        return (b, 2 * H + h)

    def o_map(g):
        b, h, i = tile(jnp.maximum(g - 1, 0))
        return (b * nq + i, h)

    return pl.pallas_call(
        kern,
        grid=(n_tiles + 1,),
        in_specs=[
            pl.BlockSpec((4, HEAD_DIM), lambda g: (0, 0)),
            pl.BlockSpec((1, V_DIM), lambda g: (0, 0)),
            pl.BlockSpec((tq, LANES), q_map),
            pl.BlockSpec((S, LANES), k_map),
            pl.BlockSpec((S, LANES), v_map),
        ],
        out_specs=pl.BlockSpec((tq, V_DIM), o_map),
        out_shape=jax.ShapeDtypeStruct((T, D_MODEL), BF16),
        scratch_shapes=[
            pltpu.VMEM((tq, 2 * S), F32),
            pltpu.VMEM((S, 2 * LANES), BF16),
            pltpu.VMEM((2 * tq, S), F32),
            pltpu.VMEM((2 * tq, S), F32),
            pltpu.VMEM((2 * tq, LANES), F32),
            pltpu.VMEM((2 * tq, LANES), F32),
        ],
        compiler_params=_params(("arbitrary",)),
        name="attn",
    )(lam_vecs, subln_g.reshape(1, V_DIM), qkv, qkv, qkv)


def _proj_kernel(x_ref, a_ref, w_ref, o_ref):
    o_ref[...] = x_ref[...] + jnp.dot(a_ref[...], w_ref[...], preferred_element_type=F32)


def proj_residual(x, a, w, *, tm):
    T, D = x.shape
    return pl.pallas_call(
        _proj_kernel,
        grid=(T // tm,),
        in_specs=[
            pl.BlockSpec((tm, D), lambda i: (i, 0)),
            pl.BlockSpec((tm, D), lambda i: (i, 0)),
            pl.BlockSpec((D, D), lambda i: (0, 0)),
        ],
        out_specs=pl.BlockSpec((tm, D), lambda i: (i, 0)),
        out_shape=jax.ShapeDtypeStruct((T, D), F32),
        compiler_params=_params(("parallel",)),
        name="proj",
    )(x, a, w)


def _mlp_kernel(x_ref, g_ref, wu_ref, wd_ref, gf_ref, o_ref, xn_ref, *, final_norm):
    j = pl.program_id(1)

    @pl.when(j == 0)
    def _():
        x = x_ref[...]
        xn_ref[...] = _rmsnorm_rows(x, g_ref[...], RMS_EPS).astype(BF16)
        o_ref[...] = x

    hdn = jnp.dot(xn_ref[...], wu_ref[...], preferred_element_type=F32)
    hdn = jnp.maximum(hdn, 0.0)
    hdn = (hdn * hdn).astype(BF16)
    o_ref[...] += jnp.dot(hdn, wd_ref[...], preferred_element_type=F32)

    if final_norm:
        @pl.when(j == pl.num_programs(1) - 1)
        def _():
            o_ref[...] = _rmsnorm_rows(o_ref[...], gf_ref[...], RMS_EPS)


def mlp_residual(x, g, w_up, w_down, g_final, *, final_norm, tm, tf):
    T, D = x.shape
    F = w_up.shape[1]
    kern = functools.partial(_mlp_kernel, final_norm=final_norm)
    return pl.pallas_call(
        kern,
        grid=(T // tm, F // tf),
        in_specs=[
            pl.BlockSpec((tm, D), lambda i, j: (i, 0)),
            pl.BlockSpec((1, D), lambda i, j: (0, 0)),
            pl.BlockSpec((D, tf), lambda i, j: (0, j)),
            pl.BlockSpec((tf, D), lambda i, j: (j, 0)),
            pl.BlockSpec((1, D), lambda i, j: (0, 0)),
        ],
        out_specs=pl.BlockSpec((tm, D), lambda i, j: (i, 0)),
        out_shape=jax.ShapeDtypeStruct((T, D), F32),
        scratch_shapes=[pltpu.VMEM((tm, D), BF16)],
        compiler_params=_params(("parallel", "arbitrary")),
        name="mlp",
    )(x, g.reshape(1, D), w_up, w_down, g_final.reshape(1, D))


def _glu_kernel(x_ref, g_ref, wa_ref, wg_ref, ba_ref, bg_ref, o_ref, xn_ref):
    @pl.when(pl.program_id(1) == 0)
    def _():
        xn_ref[...] = _rmsnorm_rows(x_ref[...], g_ref[...], RMS_EPS).astype(BF16)

    xn = xn_ref[...]
    a = jnp.dot(xn, wa_ref[...], preferred_element_type=F32) + ba_ref[...]
    gt = jnp.dot(xn, wg_ref[...], preferred_element_type=F32) + bg_ref[...]
    o_ref[...] = a * jax.nn.sigmoid(gt)


def norm_glu(x, g, w_pw1, b_pw1, *, tm, tn):
    T, D = x.shape
    nj = D // tn
    b2 = b_pw1.reshape(1, 2 * D)
    return pl.pallas_call(
        _glu_kernel,
        grid=(T // tm, nj),
        in_specs=[
            pl.BlockSpec((tm, D), lambda i, j: (i, 0)),
            pl.BlockSpec((1, D), lambda i, j: (0, 0)),
            pl.BlockSpec((D, tn), lambda i, j: (0, j)),
            pl.BlockSpec((D, tn), lambda i, j: (0, nj + j)),
            pl.BlockSpec((1, tn), lambda i, j: (0, j)),
            pl.BlockSpec((1, tn), lambda i, j: (0, nj + j)),
        ],
        out_specs=pl.BlockSpec((tm, tn), lambda i, j: (i, j)),
        out_shape=jax.ShapeDtypeStruct((T, D), F32),
        scratch_shapes=[pltpu.VMEM((tm, D), BF16)],
        compiler_params=_params(("parallel", "arbitrary")),
        name="glu",
    )(x, g.reshape(1, D), w_pw1, w_pw1, b2, b2)


HALO = 16


def _conv_kernel(x_ref, u_ref, up_ref, un_ref, wdw_ref, bdw_ref, lg_ref, lb_ref, w2_ref, b2_ref,
                 o_ref, pad_ref, y_ref, *, tm, tc, tiles_per_seq):
    i = pl.program_id(0)
    first = (i % tiles_per_seq) == 0
    last = (i % tiles_per_seq) == tiles_per_seq - 1
    pad_ref[0:HALO, :] = jnp.where(first, 0.0, up_ref[...])
    pad_ref[HALO:HALO + tm, :] = u_ref[...]
    pad_ref[HALO + tm:, :] = jnp.where(last, 0.0, un_ref[...])

    D = u_ref.shape[1]
    off = HALO - CONV_PAD

    def cols(c, carry):
        c0 = pl.multiple_of(c * tc, tc)
        w = wdw_ref[:, pl.ds(c0, tc)]
        acc = jnp.zeros((tm, tc), F32)
        for t in range(CONV_WIDTH):
            acc = acc + pad_ref[off + t:off + t + tm, pl.ds(c0, tc)] * w[t:t + 1, :]
        y_ref[:, pl.ds(c0, tc)] = acc
        return carry

    lax.fori_loop(0, D // tc, cols, 0)

    y = y_ref[...] + bdw_ref[...]
    mu = jnp.mean(y, axis=-1, keepdims=True)
    yc = y - mu
    var = jnp.mean(yc * yc, axis=-1, keepdims=True)
    z = yc * lax.rsqrt(var + LN_EPS) * lg_ref[...] + lb_ref[...]
    z = (z * jax.nn.sigmoid(z)).astype(BF16)
    o_ref[...] = x_ref[...] + jnp.dot(z, w2_ref[...], preferred_element_type=F32) + b2_ref[...]


def conv_residual(x, u, w_dw, b_dw, ln_g, ln_b, w_pw2, b_pw2, *, n_seq, tm=512, tc=256):
    T, D = x.shape
    S = T // n_seq
    tiles_per_seq = S // tm
    hb = tm // HALO
    n_hb = T // HALO
    kern = functools.partial(_conv_kernel, tm=tm, tc=tc, tiles_per_seq=tiles_per_seq)
    row = lambda v: v.reshape(1, D)
    return pl.pallas_call(
        kern,
        grid=(T // tm,),
        in_specs=[
            pl.BlockSpec((tm, D), lambda i: (i, 0)),
            pl.BlockSpec((tm, D), lambda i: (i, 0)),
            pl.BlockSpec((HALO, D), lambda i: (jnp.maximum(i * hb - 1, 0), 0)),
            pl.BlockSpec((HALO, D), lambda i: (jnp.minimum((i + 1) * hb, n_hb - 1), 0)),
            pl.BlockSpec((CONV_WIDTH, D), lambda i: (0, 0)),
            pl.BlockSpec((1, D), lambda i: (0, 0)),
            pl.BlockSpec((1, D), lambda i: (0, 0)),
            pl.BlockSpec((1, D), lambda i: (0, 0)),
            pl.BlockSpec((D, D), lambda i: (0, 0)),
            pl.BlockSpec((1, D), lambda i: (0, 0)),
        ],
        out_specs=pl.BlockSpec((tm, D), lambda i: (i, 0)),
        out_shape=jax.ShapeDtypeStruct((T, D), F32),
        scratch_shapes=[pltpu.VMEM((tm + 2 * HALO, D), F32), pltpu.VMEM((tm, D), F32)],
        compiler_params=_params(("parallel",)),
        name="conv",
    )(x, u, u, u, w_dw, row(b_dw), row(ln_g), row(ln_b), w_pw2, row(b_pw2))


def _lambda_init_for(layer_idx):
    return 0.8 - 0.6 * math.exp(-0.3 * layer_idx)


def _head_major(w):
    D = w.shape[0]
    return w.reshape(D, 2, N_HEADS, HEAD_DIM).transpose(0, 2, 1, 3).reshape(D, 2 * N_HEADS * HEAD_DIM)


def kernel(x_prompt, x_sample, attn_norm_g, w_qkv, lam_q1, lam_k1, lam_q2, lam_k2, subln_g, w_o,
           conv_norm_g, conv_w_pw1, conv_b_pw1, conv_w_dw, conv_b_dw, conv_ln_g, conv_ln_b,
           conv_w_pw2, conv_b_pw2, mlp_norm_g, w_up, w_down, final_norm_g):
    D = D_MODEL
    n_prompt = x_prompt.shape[0]
    n_seq = n_prompt + x_sample.shape[0]
    x = jnp.concatenate([x_prompt.reshape(-1, D), x_sample.reshape(-1, D)], axis=0)

    wq, wk, wv = jnp.split(w_qkv[0], 3, axis=-1)
    w_qkv_b = jnp.concatenate([_head_major(wq), _head_major(wk), wv], axis=-1).astype(BF16)
    lam_vecs = jnp.stack([lam_q1[0], lam_k1[0], lam_q2[0], lam_k2[0]]).astype(F32)
    qkv = norm_matmul(x, attn_norm_g[0], w_qkv_b, tm=1024, tn=1024)
    o = diff_attention(qkv, lam_vecs, subln_g[0], _lambda_init_for(0), n_seq=n_seq)
    x = proj_residual(x, o, w_o[0].astype(BF16), tm=1024)
    x = mlp_residual(x, mlp_norm_g[0], w_up[0].astype(BF16), w_down[0].astype(BF16), final_norm_g,
                     final_norm=False, tm=1024, tf=512)

    u = norm_glu(x, conv_norm_g[0], conv_w_pw1[0].astype(BF16), conv_b_pw1[0], tm=1024, tn=512)
    x = conv_residual(x, u, conv_w_dw[0], conv_b_dw[0], conv_ln_g[0], conv_ln_b[0],
                      conv_w_pw2[0].astype(BF16), conv_b_pw2[0], n_seq=n_seq)
    x = mlp_residual(x, mlp_norm_g[1], w_up[1].astype(BF16), w_down[1].astype(BF16), final_norm_g,
                     final_norm=True, tm=1024, tf=512)

    y_prompt = x[:n_prompt * SEQ].reshape(x_prompt.shape)
    y_sample = x[n_prompt * SEQ:].reshape(x_sample.shape)
    return (y_prompt, y_sample)
```

```python
import functools
import math

import jax
import jax.numpy as jnp
from jax import lax
from jax.experimental import pallas as pl
from jax.experimental.pallas import tpu as pltpu

D_MODEL = 2048
SEQ = 4096
HEAD_DIM = 64
N_HEADS = D_MODEL // (2 * HEAD_DIM)
V_DIM = 2 * HEAD_DIM
CONV_WIDTH = 31
CONV_PAD = (CONV_WIDTH - 1) // 2
D_FF = 4 * D_MODEL
RMS_EPS = 1e-6
SUBLN_EPS = 1e-5
LN_EPS = 1e-5

LANES = 128
SUBLANES = 8
VMEM_LIMIT = 56 * 1024 * 1024

F32 = jnp.float32
BF16 = jnp.bfloat16


def _params(semantics):
    return pltpu.CompilerParams(dimension_semantics=semantics, vmem_limit_bytes=VMEM_LIMIT)


def _rmsnorm_rows(x, g, eps):
    return x * lax.rsqrt(jnp.mean(x * x, axis=-1, keepdims=True) + eps) * g


def _split_rows_specs(tm, D, n_a, rank):
    if rank == 1:
        return [pl.BlockSpec((tm, D), lambda i: (jnp.minimum(i, n_a - 1), 0)),
                pl.BlockSpec((tm, D), lambda i: (jnp.maximum(i - n_a, 0), 0))]
    return [pl.BlockSpec((tm, D), lambda i, j: (jnp.minimum(i, n_a - 1), 0)),
            pl.BlockSpec((tm, D), lambda i, j: (jnp.maximum(i - n_a, 0), 0))]


def _norm_matmul_kernel(xa_ref, xb_ref, g_ref, w_ref, o_ref, xn_ref, *, n_a):
    i = pl.program_id(0)
    first = pl.program_id(1) == 0

    @pl.when(first & (i < n_a))
    def _():
        xn_ref[...] = _rmsnorm_rows(xa_ref[...], g_ref[...], RMS_EPS).astype(BF16)

    @pl.when(first & (i >= n_a))
    def _():
        xn_ref[...] = _rmsnorm_rows(xb_ref[...], g_ref[...], RMS_EPS).astype(BF16)

    o_ref[...] = jnp.dot(xn_ref[...], w_ref[...], preferred_element_type=F32).astype(o_ref.dtype)


def norm_matmul(xa, xb, g, w, *, tm, tn):
    D = xa.shape[1]
    T = xa.shape[0] + xb.shape[0]
    N = w.shape[1]
    n_a = xa.shape[0] // tm
    return pl.pallas_call(
        functools.partial(_norm_matmul_kernel, n_a=n_a),
        grid=(T // tm, N // tn),
        in_specs=_split_rows_specs(tm, D, n_a, 2) + [
            pl.BlockSpec((1, D), lambda i, j: (0, 0)),
            pl.BlockSpec((D, tn), lambda i, j: (0, j)),
        ],
        out_specs=pl.BlockSpec((tm, tn), lambda i, j: (i, j)),
        out_shape=jax.ShapeDtypeStruct((T, N), BF16),
        scratch_shapes=[pltpu.VMEM((tm, D), BF16)],
        compiler_params=_params(("parallel", "arbitrary")),
        name="qkv",
    )(xa, xb, g.reshape(1, D), w)


def _attn_kernel(lam_ref, g_ref, q_ref, k_ref, v_ref, o_ref,
                 bias_ref, vaug_ref, s0_ref, s1_ref, m0_ref, m1_ref, a0_ref, a1_ref,
                 *, tq, nq, n_tiles, lambda_init):
    S = k_ref.shape[0]
    g = pl.program_id(0)
    ga = jnp.minimum(g, n_tiles - 1)
    gb = jnp.clip(g - 1, 0, n_tiles - 1)
    qi = ga % nq
    h = (ga // nq) % N_HEADS

    @pl.when(g == 0)
    def _():
        s1_ref[...] = jnp.zeros_like(s1_ref)
        m1_ref[...] = jnp.zeros_like(m1_ref)
        a1_ref[...] = jnp.ones_like(a1_ref)

    @pl.when(qi == 0)
    def _():
        slope = jnp.exp2(jnp.full((1, 1), -0.5, F32) * (h + 1).astype(F32))
        r = lax.broadcasted_iota(jnp.int32, (tq, 2 * S), 0)
        j = lax.broadcasted_iota(jnp.int32, (tq, 2 * S), 1)
        bias_ref[...] = jnp.abs(r - j + S).astype(F32) * (-slope)

    @pl.when(gb % nq == 0)
    def _():
        lane = lax.broadcasted_iota(jnp.int32, (S, LANES), 1)
        vaug_ref[:, :V_DIM] = v_ref[...]
        vaug_ref[:, V_DIM:] = jnp.where(lane == 0, 1.0, 0.0).astype(BF16)

    lv = lam_ref[...]
    lam = (jnp.exp(jnp.sum(lv[0:1] * lv[1:2], axis=-1, keepdims=True))
           - jnp.exp(jnp.sum(lv[2:3] * lv[3:4], axis=-1, keepdims=True)) + lambda_init)
    b0 = pl.multiple_of(S - qi * tq, LANES)

    def step(s_w, m_w, a_w, s_r, m_r, a_r):
        q = q_ref[...]
        lane = lax.broadcasted_iota(jnp.int32, q.shape, 1)
        scale = jnp.asarray(HEAD_DIM ** -0.5, BF16)
        zero = jnp.zeros_like(q)
        qq = jnp.concatenate([jnp.where(lane < HEAD_DIM, q, zero),
                              jnp.where(lane >= HEAD_DIM, q, zero)], axis=0) * scale
        s = lax.dot_general(qq, k_ref[...], (((1,), (1,)), ((), ())), preferred_element_type=F32)
        bias = bias_ref[:, pl.ds(b0, S)]
        s = s + jnp.concatenate([bias, bias], axis=0)
        s_w[...] = s
        m_w[...] = jnp.broadcast_to(jnp.max(s, axis=-1, keepdims=True), m_w.shape)

        p = jnp.exp(s_r[...] - jnp.tile(m_r[...], (1, S // LANES))).astype(BF16)
        a_w[...] = jnp.dot(p, vaug_ref[...], preferred_element_type=F32)

        acc = a_r[...]
        o0 = acc[:tq, :V_DIM] / acc[:tq, V_DIM:V_DIM + 1]
        o1 = acc[tq:, :V_DIM] / acc[tq:, V_DIM:V_DIM + 1]
        o = o0 - lam * o1
        o = _rmsnorm_rows(o, g_ref[...], SUBLN_EPS) * (1.0 - lambda_init)
        o_ref[...] = o.astype(o_ref.dtype)

    @pl.when(g % 2 == 0)
    def _():
        step(s0_ref, m0_ref, a0_ref, s1_ref, m1_ref, a1_ref)

    @pl.when(g % 2 == 1)
    def _():
        step(s1_ref, m1_ref, a1_ref, s0_ref, m0_ref, a0_ref)


def diff_attention(qkv, lam_vecs, subln_g, lambda_init, *, n_seq, tq=256):
    T = qkv.shape[0]
    S = T // n_seq
    nq = S // tq
    H = N_HEADS
    n_tiles = n_seq * H * nq
    kern = functools.partial(_attn_kernel, tq=tq, nq=nq, n_tiles=n_tiles, lambda_init=lambda_init)

    def tile(t):
        return t // (nq * H), (t // nq) % H, t % nq

    def q_map(g):
        b, h, i = tile(jnp.minimum(g, n_tiles - 1))
        return (b * nq + i, h)

    def k_map(g):
        b, h, _ = tile(jnp.minimum(g, n_tiles - 1))
        return (b, H + h)

    def v_map(g):
        b, h, _ = tile(jnp.clip(g - 1, 0, n_tiles - 1))
        return (b, 2 * H + h)

    def o_map(g):
        b, h, i = tile(jnp.maximum(g - 2, 0))
        return (b * nq + i, h)

    return pl.pallas_call(
        kern,
        grid=(n_tiles + 2,),
        in_specs=[
            pl.BlockSpec((4, HEAD_DIM), lambda g: (0, 0)),
            pl.BlockSpec((1, V_DIM), lambda g: (0, 0)),
            pl.BlockSpec((tq, LANES), q_map),
            pl.BlockSpec((S, LANES), k_map),
            pl.BlockSpec((S, LANES), v_map),
        ],
        out_specs=pl.BlockSpec((tq, V_DIM), o_map),
        out_shape=jax.ShapeDtypeStruct((T, D_MODEL), BF16),
        scratch_shapes=[
            pltpu.VMEM((tq, 2 * S), F32),
            pltpu.VMEM((S, 2 * LANES), BF16),
            pltpu.VMEM((2 * tq, S), F32),
            pltpu.VMEM((2 * tq, S), F32),
            pltpu.VMEM((2 * tq, LANES), F32),
            pltpu.VMEM((2 * tq, LANES), F32),
            pltpu.VMEM((2 * tq, 2 * LANES), F32),
            pltpu.VMEM((2 * tq, 2 * LANES), F32),
        ],
        compiler_params=_params(("arbitrary",)),
        name="attn",
    )(lam_vecs, subln_g.reshape(1, V_DIM), qkv, qkv, qkv)


def _proj_kernel(xa_ref, xb_ref, a_ref, w_ref, o_ref, *, n_a):
    i = pl.program_id(0)
    o_ref[...] = jnp.dot(a_ref[...], w_ref[...], preferred_element_type=F32)

    @pl.when(i < n_a)
    def _():
        o_ref[...] += xa_ref[...]

    @pl.when(i >= n_a)
    def _():
        o_ref[...] += xb_ref[...]


def proj_residual(xa, xb, a, w, *, tm):
    T, D = a.shape
    n_a = xa.shape[0] // tm
    return pl.pallas_call(
        functools.partial(_proj_kernel, n_a=n_a),
        grid=(T // tm,),
        in_specs=_split_rows_specs(tm, D, n_a, 1) + [
            pl.BlockSpec((tm, D), lambda i: (i, 0)),
            pl.BlockSpec((D, D), lambda i: (0, 0)),
        ],
        out_specs=pl.BlockSpec((tm, D), lambda i: (i, 0)),
        out_shape=jax.ShapeDtypeStruct((T, D), F32),
        compiler_params=_params(("parallel",)),
        name="proj",
    )(xa, xb, a, w)


def _mlp_kernel(x_ref, g_ref, wu_ref, wd_ref, gf_ref, o_ref, xn_ref, *, final_norm):
    j = pl.program_id(1)

    @pl.when(j == 0)
    def _():
        x = x_ref[...]
        xn_ref[...] = _rmsnorm_rows(x, g_ref[...], RMS_EPS).astype(BF16)
        o_ref[...] = x

    hdn = jnp.dot(xn_ref[...], wu_ref[...], preferred_element_type=F32)
    hdn = jnp.maximum(hdn, 0.0)
    hdn = (hdn * hdn).astype(BF16)
    o_ref[...] += jnp.dot(hdn, wd_ref[...], preferred_element_type=F32)

    if final_norm:
        @pl.when(j == pl.num_programs(1) - 1)
        def _():
            o_ref[...] = _rmsnorm_rows(o_ref[...], gf_ref[...], RMS_EPS)


def mlp_residual(x, g, w_up, w_down, g_final, *, final_norm, tm, tf):
    T, D = x.shape
    F = w_up.shape[1]
    kern = functools.partial(_mlp_kernel, final_norm=final_norm)
    return pl.pallas_call(
        kern,
        grid=(T // tm, F // tf),
        in_specs=[
            pl.BlockSpec((tm, D), lambda i, j: (i, 0)),
            pl.BlockSpec((1, D), lambda i, j: (0, 0)),
            pl.BlockSpec((D, tf), lambda i, j: (0, j)),
            pl.BlockSpec((tf, D), lambda i, j: (j, 0)),
            pl.BlockSpec((1, D), lambda i, j: (0, 0)),
        ],
        out_specs=pl.BlockSpec((tm, D), lambda i, j: (i, 0)),
        out_shape=jax.ShapeDtypeStruct((T, D), F32),
        scratch_shapes=[pltpu.VMEM((tm, D), BF16)],
        compiler_params=_params(("parallel", "arbitrary")),
        name="mlp",
    )(x, g.reshape(1, D), w_up, w_down, g_final.reshape(1, D))


def _glu_kernel(x_ref, g_ref, wa_ref, wg_ref, ba_ref, bg_ref, o_ref, xn_ref):
    @pl.when(pl.program_id(1) == 0)
    def _():
        xn_ref[...] = _rmsnorm_rows(x_ref[...], g_ref[...], RMS_EPS).astype(BF16)

    xn = xn_ref[...]
    a = jnp.dot(xn, wa_ref[...], preferred_element_type=F32) + ba_ref[...]
    gt = jnp.dot(xn, wg_ref[...], preferred_element_type=F32) + bg_ref[...]
    o_ref[...] = a * jax.nn.sigmoid(gt)


def norm_glu(x, g, w_pw1, b_pw1, *, tm, tn):
    T, D = x.shape
    nj = D // tn
    b2 = b_pw1.reshape(1, 2 * D)
    return pl.pallas_call(
        _glu_kernel,
        grid=(T // tm, nj),
        in_specs=[
            pl.BlockSpec((tm, D), lambda i, j: (i, 0)),
            pl.BlockSpec((1, D), lambda i, j: (0, 0)),
            pl.BlockSpec((D, tn), lambda i, j: (0, j)),
            pl.BlockSpec((D, tn), lambda i, j: (0, nj + j)),
            pl.BlockSpec((1, tn), lambda i, j: (0, j)),
            pl.BlockSpec((1, tn), lambda i, j: (0, nj + j)),
        ],
        out_specs=pl.BlockSpec((tm, tn), lambda i, j: (i, j)),
        out_shape=jax.ShapeDtypeStruct((T, D), F32),
        scratch_shapes=[pltpu.VMEM((tm, D), BF16)],
        compiler_params=_params(("parallel", "arbitrary")),
        name="glu",
    )(x, g.reshape(1, D), w_pw1, w_pw1, b2, b2)


HALO = 16


def _conv_kernel(x_ref, u_ref, up_ref, un_ref, wdw_ref, bdw_ref, lg_ref, lb_ref, w2_ref, b2_ref,
                 o_ref, pad_ref, y_ref, xs_ref, *, tm, tc, tr, tiles_per_seq):
    i = pl.program_id(0)
    first = (i % tiles_per_seq) == 0
    last = (i % tiles_per_seq) == tiles_per_seq - 1
    pad_ref[0:HALO, :] = jnp.where(first, 0.0, up_ref[...])
    pad_ref[HALO:HALO + tm, :] = u_ref[...]
    pad_ref[HALO + tm:, :] = jnp.where(last, 0.0, un_ref[...])

    D = u_ref.shape[1]
    off = HALO - CONV_PAD

    span = xs_ref.shape[1] - tr
    n_a = span // SUBLANES + 1

    def cols(c, carry):
        c0 = pl.multiple_of(c * tc, tc)
        w = wdw_ref[:, pl.ds(c0, tc)]
        for rb in range(tm // tr):
            acc = jnp.zeros((tr, tc), F32)
            for b in range(SUBLANES):
                xs_ref[b] = pad_ref[pl.ds(rb * tr + b, tr + span), pl.ds(c0, tc)]
                for a in range(n_a):
                    t = SUBLANES * a + b - off
                    if 0 <= t < CONV_WIDTH:
                        acc = acc + xs_ref[b, SUBLANES * a:SUBLANES * a + tr, :] * w[t:t + 1, :]
            y_ref[pl.ds(rb * tr, tr), pl.ds(c0, tc)] = acc
        return carry

    lax.fori_loop(0, D // tc, cols, 0)

    y = y_ref[...] + bdw_ref[...]
    mu = jnp.mean(y, axis=-1, keepdims=True)
    yc = y - mu
    var = jnp.mean(yc * yc, axis=-1, keepdims=True)
    z = yc * lax.rsqrt(var + LN_EPS) * lg_ref[...] + lb_ref[...]
    z = (z * jax.nn.sigmoid(z)).astype(BF16)
    o_ref[...] = x_ref[...] + jnp.dot(z, w2_ref[...], preferred_element_type=F32) + b2_ref[...]


def conv_residual(x, u, w_dw, b_dw, ln_g, ln_b, w_pw2, b_pw2, *, n_seq, tm=512, tc=128, tr=64):
    T, D = x.shape
    S = T // n_seq
    tiles_per_seq = S // tm
    hb = tm // HALO
    n_hb = T // HALO
    kern = functools.partial(_conv_kernel, tm=tm, tc=tc, tr=tr, tiles_per_seq=tiles_per_seq)
    tap_span = (HALO - CONV_PAD + CONV_WIDTH - 1) // SUBLANES * SUBLANES
    row = lambda v: v.reshape(1, D)
    return pl.pallas_call(
        kern,
        grid=(T // tm,),
        in_specs=[
            pl.BlockSpec((tm, D), lambda i: (i, 0)),
            pl.BlockSpec((tm, D), lambda i: (i, 0)),
            pl.BlockSpec((HALO, D), lambda i: (jnp.maximum(i * hb - 1, 0), 0)),
            pl.BlockSpec((HALO, D), lambda i: (jnp.minimum((i + 1) * hb, n_hb - 1), 0)),
            pl.BlockSpec((CONV_WIDTH, D), lambda i: (0, 0)),
            pl.BlockSpec((1, D), lambda i: (0, 0)),
            pl.BlockSpec((1, D), lambda i: (0, 0)),
            pl.BlockSpec((1, D), lambda i: (0, 0)),
            pl.BlockSpec((D, D), lambda i: (0, 0)),
            pl.BlockSpec((1, D), lambda i: (0, 0)),
        ],
        out_specs=pl.BlockSpec((tm, D), lambda i: (i, 0)),
        out_shape=jax.ShapeDtypeStruct((T, D), F32),
        scratch_shapes=[
            pltpu.VMEM((tm + 2 * HALO, D), F32),
            pltpu.VMEM((tm, D), F32),
            pltpu.VMEM((SUBLANES, tr + tap_span, tc), F32),
        ],
        compiler_params=_params(("parallel",)),
        name="conv",
    )(x, u, u, u, w_dw, row(b_dw), row(ln_g), row(ln_b), w_pw2, row(b_pw2))


def _lambda_init_for(layer_idx):
    return 0.8 - 0.6 * math.exp(-0.3 * layer_idx)


def _head_major(w):
    D = w.shape[0]
    return w.reshape(D, 2, N_HEADS, HEAD_DIM).transpose(0, 2, 1, 3).reshape(D, 2 * N_HEADS * HEAD_DIM)


def kernel(x_prompt, x_sample, attn_norm_g, w_qkv, lam_q1, lam_k1, lam_q2, lam_k2, subln_g, w_o,
           conv_norm_g, conv_w_pw1, conv_b_pw1, conv_w_dw, conv_b_dw, conv_ln_g, conv_ln_b,
           conv_w_pw2, conv_b_pw2, mlp_norm_g, w_up, w_down, final_norm_g):
    D = D_MODEL
    n_prompt = x_prompt.shape[0]
    n_seq = n_prompt + x_sample.shape[0]
    xa = x_prompt.reshape(-1, D)
    xb = x_sample.reshape(-1, D)

    wq, wk, wv = jnp.split(w_qkv[0], 3, axis=-1)
    w_qkv_b = jnp.concatenate([_head_major(wq), _head_major(wk), wv], axis=-1).astype(BF16)
    lam_vecs = jnp.stack([lam_q1[0], lam_k1[0], lam_q2[0], lam_k2[0]]).astype(F32)
    qkv = norm_matmul(xa, xb, attn_norm_g[0], w_qkv_b, tm=1024, tn=1024)
    o = diff_attention(qkv, lam_vecs, subln_g[0], _lambda_init_for(0), n_seq=n_seq)
    x = proj_residual(xa, xb, o, w_o[0].astype(BF16), tm=512)
    x = mlp_residual(x, mlp_norm_g[0], w_up[0].astype(BF16), w_down[0].astype(BF16), final_norm_g,
                     final_norm=False, tm=1024, tf=512)

    u = norm_glu(x, conv_norm_g[0], conv_w_pw1[0].astype(BF16), conv_b_pw1[0], tm=1024, tn=512)
    x = conv_residual(x, u, conv_w_dw[0], conv_b_dw[0], conv_ln_g[0], conv_ln_b[0],
                      conv_w_pw2[0].astype(BF16), conv_b_pw2[0], n_seq=n_seq)
    x = mlp_residual(x, mlp_norm_g[1], w_up[1].astype(BF16), w_down[1].astype(BF16), final_norm_g,
                     final_norm=True, tm=1024, tf=512)

    y_prompt = x[:n_prompt * SEQ].reshape(x_prompt.shape)
    y_sample = x[n_prompt * SEQ:].reshape(x_sample.shape)
    return (y_prompt, y_sample)
```

```python
import functools
import math

import jax
import jax.numpy as jnp
from jax import lax
from jax.experimental import pallas as pl
from jax.experimental.pallas import tpu as pltpu

D_MODEL = 2048
SEQ = 4096
HEAD_DIM = 64
N_HEADS = D_MODEL // (2 * HEAD_DIM)
V_DIM = 2 * HEAD_DIM
CONV_WIDTH = 31
CONV_PAD = (CONV_WIDTH - 1) // 2
D_FF = 4 * D_MODEL
RMS_EPS = 1e-6
SUBLN_EPS = 1e-5
LN_EPS = 1e-5

LANES = 128
SUBLANES = 8
VMEM_LIMIT = 56 * 1024 * 1024

F32 = jnp.float32
BF16 = jnp.bfloat16


def _params(semantics):
    return pltpu.CompilerParams(dimension_semantics=semantics, vmem_limit_bytes=VMEM_LIMIT)


def _rmsnorm_rows(x, g, eps):
    return x * lax.rsqrt(jnp.mean(x * x, axis=-1, keepdims=True) + eps) * g


def _split_rows_specs(tm, D, n_a, rank):
    if rank == 1:
        return [pl.BlockSpec((tm, D), lambda i: (jnp.minimum(i, n_a - 1), 0)),
                pl.BlockSpec((tm, D), lambda i: (jnp.maximum(i - n_a, 0), 0))]
    return [pl.BlockSpec((tm, D), lambda i, j: (jnp.minimum(i, n_a - 1), 0)),
            pl.BlockSpec((tm, D), lambda i, j: (jnp.maximum(i - n_a, 0), 0))]


def _norm_matmul_kernel(xa_ref, xb_ref, g_ref, w_ref, o_ref, xn_ref, *, n_a):
    i = pl.program_id(0)
    first = pl.program_id(1) == 0

    @pl.when(first & (i < n_a))
    def _():
        xn_ref[...] = _rmsnorm_rows(xa_ref[...], g_ref[...], RMS_EPS).astype(BF16)

    @pl.when(first & (i >= n_a))
    def _():
        xn_ref[...] = _rmsnorm_rows(xb_ref[...], g_ref[...], RMS_EPS).astype(BF16)

    o_ref[...] = jnp.dot(xn_ref[...], w_ref[...], preferred_element_type=F32).astype(o_ref.dtype)


def norm_matmul(xa, xb, g, w, *, tm, tn):
    D = xa.shape[1]
    T = xa.shape[0] + xb.shape[0]
    N = w.shape[1]
    n_a = xa.shape[0] // tm
    return pl.pallas_call(
        functools.partial(_norm_matmul_kernel, n_a=n_a),
        grid=(T // tm, N // tn),
        in_specs=_split_rows_specs(tm, D, n_a, 2) + [
            pl.BlockSpec((1, D), lambda i, j: (0, 0)),
            pl.BlockSpec((D, tn), lambda i, j: (0, j)),
        ],
        out_specs=pl.BlockSpec((tm, tn), lambda i, j: (i, j)),
        out_shape=jax.ShapeDtypeStruct((T, N), BF16),
        scratch_shapes=[pltpu.VMEM((tm, D), BF16)],
        compiler_params=_params(("parallel", "arbitrary")),
        name="qkv",
    )(xa, xb, g.reshape(1, D), w)


def _attn_kernel(lam_ref, g_ref, q_ref, k_ref, v_ref, o_ref,
                 bias_ref, vaug_ref, s0_ref, s1_ref, m0_ref, m1_ref, a0_ref, a1_ref,
                 *, tq, nq, n_tiles, lambda_init):
    S = k_ref.shape[0]
    g = pl.program_id(0)
    ga = jnp.minimum(g, n_tiles - 1)
    gb = jnp.clip(g - 1, 0, n_tiles - 1)
    qi = ga % nq
    h = (ga // nq) % N_HEADS

    @pl.when(g == 0)
    def _():
        s1_ref[...] = jnp.zeros_like(s1_ref)
        m1_ref[...] = jnp.zeros_like(m1_ref)
        a1_ref[...] = jnp.ones_like(a1_ref)
        r = lax.broadcasted_iota(jnp.int32, (tq, 2 * S), 0)
        j = lax.broadcasted_iota(jnp.int32, (tq, 2 * S), 1)
        bias_ref[...] = jnp.abs(r - j + S).astype(F32)

    neg_slope = -jnp.exp2(jnp.full((1, 1), -0.5, F32) * (h + 1).astype(F32))

    @pl.when(gb % nq == 0)
    def _():
        lane = lax.broadcasted_iota(jnp.int32, (S, LANES), 1)
        vaug_ref[:, :V_DIM] = v_ref[...]
        vaug_ref[:, V_DIM:] = jnp.where(lane == 0, 1.0, 0.0).astype(BF16)

    lv = lam_ref[...]
    lam = (jnp.exp(jnp.sum(lv[0:1] * lv[1:2], axis=-1, keepdims=True))
           - jnp.exp(jnp.sum(lv[2:3] * lv[3:4], axis=-1, keepdims=True)) + lambda_init)
    b0 = pl.multiple_of(S - qi * tq, LANES)

    def step(s_w, m_w, a_w, s_r, m_r, a_r):
        q = q_ref[...]
        lane = lax.broadcasted_iota(jnp.int32, q.shape, 1)
        scale = jnp.asarray(HEAD_DIM ** -0.5, BF16)
        zero = jnp.zeros_like(q)
        qq = jnp.concatenate([jnp.where(lane < HEAD_DIM, q, zero),
                              jnp.where(lane >= HEAD_DIM, q, zero)], axis=0) * scale
        s = lax.dot_general(qq, k_ref[...], (((1,), (1,)), ((), ())), preferred_element_type=F32)
        bias = bias_ref[:, pl.ds(b0, S)] * neg_slope
        s = s + jnp.concatenate([bias, bias], axis=0)
        s_w[...] = s
        mp = s[:, :LANES]
        for c in range(1, S // LANES):
            mp = jnp.maximum(mp, s[:, c * LANES:(c + 1) * LANES])
        m_w[...] = mp

        m = jnp.max(m_r[...], axis=-1, keepdims=True)
        p = jnp.exp(s_r[...] - m).astype(BF16)
        a_w[...] = jnp.dot(p, vaug_ref[...], preferred_element_type=F32)

        acc = a_r[...]
        o0 = acc[:tq, :V_DIM] / acc[:tq, V_DIM:V_DIM + 1]
        o1 = acc[tq:, :V_DIM] / acc[tq:, V_DIM:V_DIM + 1]
        o = o0 - lam * o1
        o = _rmsnorm_rows(o, g_ref[...], SUBLN_EPS) * (1.0 - lambda_init)
        o_ref[...] = o.astype(o_ref.dtype)

    @pl.when(g % 2 == 0)
    def _():
        step(s0_ref, m0_ref, a0_ref, s1_ref, m1_ref, a1_ref)

    @pl.when(g % 2 == 1)
    def _():
        step(s1_ref, m1_ref, a1_ref, s0_ref, m0_ref, a0_ref)


def diff_attention(qkv, lam_vecs, subln_g, lambda_init, *, n_seq, tq=256):
    T = qkv.shape[0]
    S = T // n_seq
    nq = S // tq
    H = N_HEADS
    n_tiles = n_seq * H * nq
    kern = functools.partial(_attn_kernel, tq=tq, nq=nq, n_tiles=n_tiles, lambda_init=lambda_init)

    def tile(t):
        return t // (nq * H), (t // nq) % H, t % nq

    def q_map(g):
        b, h, i = tile(jnp.minimum(g, n_tiles - 1))
        return (b * nq + i, h)

    def k_map(g):
        b, h, _ = tile(jnp.minimum(g, n_tiles - 1))
        return (b, H + h)

    def v_map(g):
        b, h, _ = tile(jnp.clip(g - 1, 0, n_tiles - 1))
        return (b, 2 * H + h)

    def o_map(g):
        b, h, i = tile(jnp.maximum(g - 2, 0))
        return (b * nq + i, h)

    return pl.pallas_call(
        kern,
        grid=(n_tiles + 2,),
        in_specs=[
            pl.BlockSpec((4, HEAD_DIM), lambda g: (0, 0)),
            pl.BlockSpec((1, V_DIM), lambda g: (0, 0)),
            pl.BlockSpec((tq, LANES), q_map),
            pl.BlockSpec((S, LANES), k_map),
            pl.BlockSpec((S, LANES), v_map),
        ],
        out_specs=pl.BlockSpec((tq, V_DIM), o_map),
        out_shape=jax.ShapeDtypeStruct((T, D_MODEL), BF16),
        scratch_shapes=[
            pltpu.VMEM((tq, 2 * S), F32),
            pltpu.VMEM((S, 2 * LANES), BF16),
            pltpu.VMEM((2 * tq, S), F32),
            pltpu.VMEM((2 * tq, S), F32),
            pltpu.VMEM((2 * tq, LANES), F32),
            pltpu.VMEM((2 * tq, LANES), F32),
            pltpu.VMEM((2 * tq, 2 * LANES), F32),
            pltpu.VMEM((2 * tq, 2 * LANES), F32),
        ],
        compiler_params=_params(("arbitrary",)),
        name="attn",
    )(lam_vecs, subln_g.reshape(1, V_DIM), qkv, qkv, qkv)


def _proj_kernel(xa_ref, xb_ref, a_ref, w_ref, o_ref, *, n_a):
    i = pl.program_id(0)
    o_ref[...] = jnp.dot(a_ref[...], w_ref[...], preferred_element_type=F32)

    @pl.when(i < n_a)
    def _():
        o_ref[...] += xa_ref[...]

    @pl.when(i >= n_a)
    def _():
        o_ref[...] += xb_ref[...]


def proj_residual(xa, xb, a, w, *, tm):
    T, D = a.shape
    n_a = xa.shape[0] // tm
    return pl.pallas_call(
        functools.partial(_proj_kernel, n_a=n_a),
        grid=(T // tm,),
        in_specs=_split_rows_specs(tm, D, n_a, 1) + [
            pl.BlockSpec((tm, D), lambda i: (i, 0)),
            pl.BlockSpec((D, D), lambda i: (0, 0)),
        ],
        out_specs=pl.BlockSpec((tm, D), lambda i: (i, 0)),
        out_shape=jax.ShapeDtypeStruct((T, D), F32),
        compiler_params=_params(("parallel",)),
        name="proj",
    )(xa, xb, a, w)


def _mlp_kernel(x_ref, g_ref, wu_ref, wd_ref, gf_ref, o_ref, xn_ref, *, final_norm):
    j = pl.program_id(1)

    @pl.when(j == 0)
    def _():
        x = x_ref[...]
        xn_ref[...] = _rmsnorm_rows(x, g_ref[...], RMS_EPS).astype(BF16)
        o_ref[...] = x

    hdn = jnp.dot(xn_ref[...], wu_ref[...], preferred_element_type=F32)
    hdn = jnp.maximum(hdn, 0.0)
    hdn = (hdn * hdn).astype(BF16)
    o_ref[...] += jnp.dot(hdn, wd_ref[...], preferred_element_type=F32)

    if final_norm:
        @pl.when(j == pl.num_programs(1) - 1)
        def _():
            o_ref[...] = _rmsnorm_rows(o_ref[...], gf_ref[...], RMS_EPS)


def mlp_residual(x, g, w_up, w_down, g_final, *, layer, final_norm, tm, tf, row_start=0, n_rows=None):
    D = x.shape[1]
    n_rows = x.shape[0] if n_rows is None else n_rows
    F = w_up.shape[2]
    i0 = row_start // tm
    kern = functools.partial(_mlp_kernel, final_norm=final_norm)
    return pl.pallas_call(
        kern,
        grid=(n_rows // tm, F // tf),
        in_specs=[
            pl.BlockSpec((tm, D), lambda i, j: (i0 + i, 0)),
            pl.BlockSpec((1, D), lambda i, j: (0, 0)),
            pl.BlockSpec((None, D, tf), lambda i, j: (layer, 0, j)),
            pl.BlockSpec((None, tf, D), lambda i, j: (layer, j, 0)),
            pl.BlockSpec((1, D), lambda i, j: (0, 0)),
        ],
        out_specs=pl.BlockSpec((tm, D), lambda i, j: (i, 0)),
        out_shape=jax.ShapeDtypeStruct((n_rows, D), F32),
        scratch_shapes=[pltpu.VMEM((tm, D), BF16)],
        compiler_params=_params(("parallel", "arbitrary")),
        name="mlp",
    )(x, g.reshape(1, D), w_up, w_down, g_final.reshape(1, D))


def _glu_kernel(x_ref, g_ref, wa_ref, wg_ref, ba_ref, bg_ref, o_ref, xn_ref):
    @pl.when(pl.program_id(1) == 0)
    def _():
        xn_ref[...] = _rmsnorm_rows(x_ref[...], g_ref[...], RMS_EPS).astype(BF16)

    xn = xn_ref[...]
    a = jnp.dot(xn, wa_ref[...], preferred_element_type=F32) + ba_ref[...]
    gt = jnp.dot(xn, wg_ref[...], preferred_element_type=F32) + bg_ref[...]
    o_ref[...] = a * jax.nn.sigmoid(gt)


def norm_glu(x, g, w_pw1, b_pw1, *, tm, tn):
    T, D = x.shape
    nj = D // tn
    b2 = b_pw1.reshape(1, 2 * D)
    return pl.pallas_call(
        _glu_kernel,
        grid=(T // tm, nj),
        in_specs=[
            pl.BlockSpec((tm, D), lambda i, j: (i, 0)),
            pl.BlockSpec((1, D), lambda i, j: (0, 0)),
            pl.BlockSpec((D, tn), lambda i, j: (0, j)),
            pl.BlockSpec((D, tn), lambda i, j: (0, nj + j)),
            pl.BlockSpec((1, tn), lambda i, j: (0, j)),
            pl.BlockSpec((1, tn), lambda i, j: (0, nj + j)),
        ],
        out_specs=pl.BlockSpec((tm, tn), lambda i, j: (i, j)),
        out_shape=jax.ShapeDtypeStruct((T, D), F32),
        scratch_shapes=[pltpu.VMEM((tm, D), BF16)],
        compiler_params=_params(("parallel", "arbitrary")),
        name="glu",
    )(x, g.reshape(1, D), w_pw1, w_pw1, b2, b2)


HALO = 16


def _conv_kernel(x_ref, u_ref, up_ref, un_ref, wdw_ref, bdw_ref, lg_ref, lb_ref, w2_ref, b2_ref,
                 o_ref, pad_ref, y_ref, xs_ref, *, tm, tc, tr, tiles_per_seq):
    i = pl.program_id(0)
    first = (i % tiles_per_seq) == 0
    last = (i % tiles_per_seq) == tiles_per_seq - 1
    pad_ref[0:HALO, :] = jnp.where(first, 0.0, up_ref[...])
    pad_ref[HALO:HALO + tm, :] = u_ref[...]
    pad_ref[HALO + tm:, :] = jnp.where(last, 0.0, un_ref[...])

    D = u_ref.shape[1]
    off = HALO - CONV_PAD

    span = xs_ref.shape[1] - tr
    n_a = span // SUBLANES + 1

    def cols(c, carry):
        c0 = pl.multiple_of(c * tc, tc)
        w = wdw_ref[:, pl.ds(c0, tc)]
        for rb in range(tm // tr):
            acc = jnp.zeros((tr, tc), F32)
            for b in range(SUBLANES):
                xs_ref[b] = pad_ref[pl.ds(rb * tr + b, tr + span), pl.ds(c0, tc)]
                for a in range(n_a):
                    t = SUBLANES * a + b - off
                    if 0 <= t < CONV_WIDTH:
                        acc = acc + xs_ref[b, SUBLANES * a:SUBLANES * a + tr, :] * w[t:t + 1, :]
            y_ref[pl.ds(rb * tr, tr), pl.ds(c0, tc)] = acc
        return carry

    lax.fori_loop(0, D // tc, cols, 0)

    y = y_ref[...] + bdw_ref[...]
    mu = jnp.mean(y, axis=-1, keepdims=True)
    yc = y - mu
    var = jnp.mean(yc * yc, axis=-1, keepdims=True)
    z = yc * lax.rsqrt(var + LN_EPS) * lg_ref[...] + lb_ref[...]
    z = (z * jax.nn.sigmoid(z)).astype(BF16)
    o_ref[...] = x_ref[...] + jnp.dot(z, w2_ref[...], preferred_element_type=F32) + b2_ref[...]


def conv_residual(x, u, w_dw, b_dw, ln_g, ln_b, w_pw2, b_pw2, *, n_seq, tm=512, tc=128, tr=128):
    T, D = x.shape
    S = T // n_seq
    tiles_per_seq = S // tm
    hb = tm // HALO
    n_hb = T // HALO
    kern = functools.partial(_conv_kernel, tm=tm, tc=tc, tr=tr, tiles_per_seq=tiles_per_seq)
    tap_span = (HALO - CONV_PAD + CONV_WIDTH - 1) // SUBLANES * SUBLANES
    row = lambda v: v.reshape(1, D)
    return pl.pallas_call(
        kern,
        grid=(T // tm,),
        in_specs=[
            pl.BlockSpec((tm, D), lambda i: (i, 0)),
            pl.BlockSpec((tm, D), lambda i: (i, 0)),
            pl.BlockSpec((HALO, D), lambda i: (jnp.maximum(i * hb - 1, 0), 0)),
            pl.BlockSpec((HALO, D), lambda i: (jnp.minimum((i + 1) * hb, n_hb - 1), 0)),
            pl.BlockSpec((CONV_WIDTH, D), lambda i: (0, 0)),
            pl.BlockSpec((1, D), lambda i: (0, 0)),
            pl.BlockSpec((1, D), lambda i: (0, 0)),
            pl.BlockSpec((1, D), lambda i: (0, 0)),
            pl.BlockSpec((D, D), lambda i: (0, 0)),
            pl.BlockSpec((1, D), lambda i: (0, 0)),
        ],
        out_specs=pl.BlockSpec((tm, D), lambda i: (i, 0)),
        out_shape=jax.ShapeDtypeStruct((T, D), F32),
        scratch_shapes=[
            pltpu.VMEM((tm + 2 * HALO, D), F32),
            pltpu.VMEM((tm, D), F32),
            pltpu.VMEM((SUBLANES, tr + tap_span, tc), F32),
        ],
        compiler_params=_params(("parallel",)),
        name="conv",
    )(x, u, u, u, w_dw, row(b_dw), row(ln_g), row(ln_b), w_pw2, row(b_pw2))


def _lambda_init_for(layer_idx):
    return 0.8 - 0.6 * math.exp(-0.3 * layer_idx)


def _head_major(w):
    D = w.shape[0]
    return w.reshape(D, 2, N_HEADS, HEAD_DIM).transpose(0, 2, 1, 3).reshape(D, 2 * N_HEADS * HEAD_DIM)


def kernel(x_prompt, x_sample, attn_norm_g, w_qkv, lam_q1, lam_k1, lam_q2, lam_k2, subln_g, w_o,
           conv_norm_g, conv_w_pw1, conv_b_pw1, conv_w_dw, conv_b_dw, conv_ln_g, conv_ln_b,
           conv_w_pw2, conv_b_pw2, mlp_norm_g, w_up, w_down, final_norm_g):
    D = D_MODEL
    n_prompt = x_prompt.shape[0]
    n_seq = n_prompt + x_sample.shape[0]
    xa = x_prompt.reshape(-1, D)
    xb = x_sample.reshape(-1, D)

    wq, wk, wv = jnp.split(w_qkv[0], 3, axis=-1)
    w_qkv_b = jnp.concatenate([_head_major(wq), _head_major(wk), wv], axis=-1).astype(BF16)
    lam_vecs = jnp.stack([lam_q1[0], lam_k1[0], lam_q2[0], lam_k2[0]]).astype(F32)
    qkv = norm_matmul(xa, xb, attn_norm_g[0], w_qkv_b, tm=1024, tn=1024)
    o = diff_attention(qkv, lam_vecs, subln_g[0], _lambda_init_for(0), n_seq=n_seq)
    x = proj_residual(xa, xb, o, w_o[0].astype(BF16), tm=512)
    w_up_b = w_up.astype(BF16)
    w_down_b = w_down.astype(BF16)
    x = mlp_residual(x, mlp_norm_g[0], w_up_b, w_down_b, final_norm_g,
                     layer=0, final_norm=False, tm=1024, tf=512)

    u = norm_glu(x, conv_norm_g[0], conv_w_pw1[0].astype(BF16), conv_b_pw1[0], tm=1024, tn=512)
    x = conv_residual(x, u, conv_w_dw[0], conv_b_dw[0], conv_ln_g[0], conv_ln_b[0],
                      conv_w_pw2[0].astype(BF16), conv_b_pw2[0], n_seq=n_seq)
    last = functools.partial(mlp_residual, x, mlp_norm_g[1], w_up_b, w_down_b, final_norm_g,
                             layer=1, final_norm=True, tm=1024, tf=512)
    y_prompt = last(row_start=0, n_rows=xa.shape[0])
    y_sample = last(row_start=xa.shape[0], n_rows=xb.shape[0])
    return (y_prompt.reshape(x_prompt.shape), y_sample.reshape(x_sample.shape))
```

```python
import functools
import math

import jax
import jax.numpy as jnp
from jax import lax
from jax.experimental import pallas as pl
from jax.experimental.pallas import tpu as pltpu

D_MODEL = 2048
SEQ = 4096
HEAD_DIM = 64
N_HEADS = D_MODEL // (2 * HEAD_DIM)
V_DIM = 2 * HEAD_DIM
CONV_WIDTH = 31
CONV_PAD = (CONV_WIDTH - 1) // 2
D_FF = 4 * D_MODEL
RMS_EPS = 1e-6
SUBLN_EPS = 1e-5
LN_EPS = 1e-5

LANES = 128
SUBLANES = 8
VMEM_LIMIT = 56 * 1024 * 1024

F32 = jnp.float32
BF16 = jnp.bfloat16


ATTN_VMEM_LIMIT = 62 * 1024 * 1024
MLP_VMEM_LIMIT = 62 * 1024 * 1024


def _params(semantics, vmem_limit=VMEM_LIMIT):
    return pltpu.CompilerParams(dimension_semantics=semantics, vmem_limit_bytes=vmem_limit)


def _rmsnorm_rows(x, g, eps):
    return x * lax.rsqrt(jnp.mean(x * x, axis=-1, keepdims=True) + eps) * g


def _split_rows_specs(tm, D, n_a, rank):
    if rank == 1:
        return [pl.BlockSpec((tm, D), lambda i: (jnp.minimum(i, n_a - 1), 0)),
                pl.BlockSpec((tm, D), lambda i: (jnp.maximum(i - n_a, 0), 0))]
    return [pl.BlockSpec((tm, D), lambda i, j: (jnp.minimum(i, n_a - 1), 0)),
            pl.BlockSpec((tm, D), lambda i, j: (jnp.maximum(i - n_a, 0), 0))]


def _norm_matmul_kernel(xa_ref, xb_ref, g_ref, w_ref, o_ref, xn_ref, *, n_a):
    i = pl.program_id(0)
    first = pl.program_id(1) == 0

    @pl.when(first & (i < n_a))
    def _():
        xn_ref[...] = _rmsnorm_rows(xa_ref[...], g_ref[...], RMS_EPS).astype(BF16)

    @pl.when(first & (i >= n_a))
    def _():
        xn_ref[...] = _rmsnorm_rows(xb_ref[...], g_ref[...], RMS_EPS).astype(BF16)

    o_ref[...] = jnp.dot(xn_ref[...], w_ref[...], preferred_element_type=F32).astype(o_ref.dtype)


def norm_matmul(xa, xb, g, w, *, tm, tn):
    D = xa.shape[1]
    T = xa.shape[0] + xb.shape[0]
    N = w.shape[1]
    n_a = xa.shape[0] // tm
    return pl.pallas_call(
        functools.partial(_norm_matmul_kernel, n_a=n_a),
        grid=(T // tm, N // tn),
        in_specs=_split_rows_specs(tm, D, n_a, 2) + [
            pl.BlockSpec((1, D), lambda i, j: (0, 0)),
            pl.BlockSpec((D, tn), lambda i, j: (0, j)),
        ],
        out_specs=pl.BlockSpec((tm, tn), lambda i, j: (i, j)),
        out_shape=jax.ShapeDtypeStruct((T, N), BF16),
        scratch_shapes=[pltpu.VMEM((tm, D), BF16)],
        compiler_params=_params(("parallel", "arbitrary")),
        name="qkv",
    )(xa, xb, g.reshape(1, D), w)


def _attn_kernel(lam_ref, g_ref, q_ref, k_ref, v_ref, o_ref,
                 bias_ref, vaug_ref, s0_ref, s1_ref, m0_ref, m1_ref, a0_ref, a1_ref,
                 *, tq, tb, nq, n_tiles, lambda_init):
    S = k_ref.shape[0]
    g = pl.program_id(0)
    ga = jnp.minimum(g, n_tiles - 1)
    gb = jnp.clip(g - 1, 0, n_tiles - 1)
    qi = ga % nq
    h = (ga // nq) % N_HEADS

    @pl.when(g == 0)
    def _():
        s1_ref[...] = jnp.zeros_like(s1_ref)
        m1_ref[...] = jnp.zeros_like(m1_ref)
        a1_ref[...] = jnp.ones_like(a1_ref)
        r = lax.broadcasted_iota(jnp.int32, (tb, 2 * S), 0)
        j = lax.broadcasted_iota(jnp.int32, (tb, 2 * S), 1)
        bias_ref[...] = jnp.abs(r - j + S).astype(F32)

    neg_slope = -jnp.exp2(jnp.full((1, 1), -0.5, F32) * (h + 1).astype(F32))

    @pl.when(gb % nq == 0)
    def _():
        lane = lax.broadcasted_iota(jnp.int32, (S, LANES), 1)
        vaug_ref[:, :V_DIM] = v_ref[...]
        vaug_ref[:, V_DIM:] = jnp.where(lane == 0, 1.0, 0.0).astype(BF16)

    lv = lam_ref[...]
    lam = (jnp.exp(jnp.sum(lv[0:1] * lv[1:2], axis=-1, keepdims=True))
           - jnp.exp(jnp.sum(lv[2:3] * lv[3:4], axis=-1, keepdims=True)) + lambda_init)
    b0 = pl.multiple_of(S - qi * tq, LANES)

    def step(s_w, m_w, a_w, s_r, m_r, a_r):
        q = q_ref[...]
        lane = lax.broadcasted_iota(jnp.int32, q.shape, 1)
        scale = jnp.asarray(HEAD_DIM ** -0.5, BF16)
        zero = jnp.zeros_like(q)
        qq = jnp.concatenate([jnp.where(lane < HEAD_DIM, q, zero),
                              jnp.where(lane >= HEAD_DIM, q, zero)], axis=0) * scale
        s = lax.dot_general(qq, k_ref[...], (((1,), (1,)), ((), ())), preferred_element_type=F32)
        bias = jnp.concatenate(
            [bias_ref[:, pl.ds(pl.multiple_of(b0 - r0, LANES), S)] for r0 in range(0, tq, tb)],
            axis=0) * neg_slope
        s = s + jnp.concatenate([bias, bias], axis=0)
        s_w[...] = s
        mp = s[:, :LANES]
        for c in range(1, S // LANES):
            mp = jnp.maximum(mp, s[:, c * LANES:(c + 1) * LANES])
        m_w[...] = mp

        m = jnp.max(m_r[...], axis=-1, keepdims=True)
        p = jnp.exp(s_r[...] - m).astype(BF16)
        a_w[...] = jnp.dot(p, vaug_ref[...], preferred_element_type=F32)

        acc = a_r[...]
        o0 = acc[:tq, :V_DIM] / acc[:tq, V_DIM:V_DIM + 1]
        o1 = acc[tq:, :V_DIM] / acc[tq:, V_DIM:V_DIM + 1]
        o = o0 - lam * o1
        o = _rmsnorm_rows(o, g_ref[...], SUBLN_EPS) * (1.0 - lambda_init)
        o_ref[...] = o.astype(o_ref.dtype)

    @pl.when(g % 2 == 0)
    def _():
        step(s0_ref, m0_ref, a0_ref, s1_ref, m1_ref, a1_ref)

    @pl.when(g % 2 == 1)
    def _():
        step(s1_ref, m1_ref, a1_ref, s0_ref, m0_ref, a0_ref)


def diff_attention(qkv, lam_vecs, subln_g, lambda_init, *, n_seq, tq=512, tb=256):
    T = qkv.shape[0]
    S = T // n_seq
    nq = S // tq
    H = N_HEADS
    n_tiles = n_seq * H * nq
    kern = functools.partial(_attn_kernel, tq=tq, tb=tb, nq=nq, n_tiles=n_tiles, lambda_init=lambda_init)

    def tile(t):
        return t // (nq * H), (t // nq) % H, t % nq

    def q_map(g):
        b, h, i = tile(jnp.minimum(g, n_tiles - 1))
        return (b * nq + i, h)

    def k_map(g):
        b, h, _ = tile(jnp.minimum(g, n_tiles - 1))
        return (b, H + h)

    def v_map(g):
        b, h, _ = tile(jnp.clip(g - 1, 0, n_tiles - 1))
        return (b, 2 * H + h)

    def o_map(g):
        b, h, i = tile(jnp.maximum(g - 2, 0))
        return (b * nq + i, h)

    return pl.pallas_call(
        kern,
        grid=(n_tiles + 2,),
        in_specs=[
            pl.BlockSpec((4, HEAD_DIM), lambda g: (0, 0)),
            pl.BlockSpec((1, V_DIM), lambda g: (0, 0)),
            pl.BlockSpec((tq, LANES), q_map),
            pl.BlockSpec((S, LANES), k_map),
            pl.BlockSpec((S, LANES), v_map),
        ],
        out_specs=pl.BlockSpec((tq, V_DIM), o_map),
        out_shape=jax.ShapeDtypeStruct((T, D_MODEL), BF16),
        scratch_shapes=[
            pltpu.VMEM((tb, 2 * S), F32),
            pltpu.VMEM((S, 2 * LANES), BF16),
            pltpu.VMEM((2 * tq, S), F32),
            pltpu.VMEM((2 * tq, S), F32),
            pltpu.VMEM((2 * tq, LANES), F32),
            pltpu.VMEM((2 * tq, LANES), F32),
            pltpu.VMEM((2 * tq, 2 * LANES), F32),
            pltpu.VMEM((2 * tq, 2 * LANES), F32),
        ],
        compiler_params=_params(("arbitrary",), ATTN_VMEM_LIMIT),
        name="attn",
    )(lam_vecs, subln_g.reshape(1, V_DIM), qkv, qkv, qkv)


def _proj_kernel(xa_ref, xb_ref, a_ref, w_ref, o_ref, *, n_a):
    i = pl.program_id(0)
    o_ref[...] = jnp.dot(a_ref[...], w_ref[...], preferred_element_type=F32)

    @pl.when(i < n_a)
    def _():
        o_ref[...] += xa_ref[...]

    @pl.when(i >= n_a)
    def _():
        o_ref[...] += xb_ref[...]


def proj_residual(xa, xb, a, w, *, tm):
    T, D = a.shape
    n_a = xa.shape[0] // tm
    return pl.pallas_call(
        functools.partial(_proj_kernel, n_a=n_a),
        grid=(T // tm,),
        in_specs=_split_rows_specs(tm, D, n_a, 1) + [
            pl.BlockSpec((tm, D), lambda i: (i, 0)),
            pl.BlockSpec((D, D), lambda i: (0, 0)),
        ],
        out_specs=pl.BlockSpec((tm, D), lambda i: (i, 0)),
        out_shape=jax.ShapeDtypeStruct((T, D), F32),
        compiler_params=_params(("parallel",)),
        name="proj",
    )(xa, xb, a, w)


def _mlp_kernel(x_ref, g_ref, wu_ref, wd_ref, gf_ref, o_ref, xn_ref, *, final_norm):
    j = pl.program_id(1)

    @pl.when(j == 0)
    def _():
        x = x_ref[...]
        xn_ref[...] = _rmsnorm_rows(x, g_ref[...], RMS_EPS).astype(BF16)
        o_ref[...] = x

    hdn = jnp.dot(xn_ref[...], wu_ref[...], preferred_element_type=F32)
    hdn = jnp.maximum(hdn, 0.0)
    hdn = (hdn * hdn).astype(BF16)
    o_ref[...] += jnp.dot(hdn, wd_ref[...], preferred_element_type=F32)

    if final_norm:
        @pl.when(j == pl.num_programs(1) - 1)
        def _():
            o_ref[...] = _rmsnorm_rows(o_ref[...], gf_ref[...], RMS_EPS)


def mlp_residual(x, g, w_up, w_down, g_final, *, layer, final_norm, tm, tf, row_start=0, n_rows=None):
    D = x.shape[1]
    n_rows = x.shape[0] if n_rows is None else n_rows
    F = w_up.shape[2]
    i0 = row_start // tm
    kern = functools.partial(_mlp_kernel, final_norm=final_norm)
    return pl.pallas_call(
        kern,
        grid=(n_rows // tm, F // tf),
        in_specs=[
            pl.BlockSpec((tm, D), lambda i, j: (i0 + i, 0)),
            pl.BlockSpec((1, D), lambda i, j: (0, 0)),
            pl.BlockSpec((None, D, tf), lambda i, j: (layer, 0, j)),
            pl.BlockSpec((None, tf, D), lambda i, j: (layer, j, 0)),
            pl.BlockSpec((1, D), lambda i, j: (0, 0)),
        ],
        out_specs=pl.BlockSpec((tm, D), lambda i, j: (i, 0)),
        out_shape=jax.ShapeDtypeStruct((n_rows, D), F32),
        scratch_shapes=[pltpu.VMEM((tm, D), BF16)],
        compiler_params=_params(("parallel", "arbitrary"), MLP_VMEM_LIMIT),
        name="mlp",
    )(x, g.reshape(1, D), w_up, w_down, g_final.reshape(1, D))


def _glu_kernel(x_ref, g_ref, wa_ref, wg_ref, ba_ref, bg_ref, o_ref, xn_ref):
    @pl.when(pl.program_id(1) == 0)
    def _():
        xn_ref[...] = _rmsnorm_rows(x_ref[...], g_ref[...], RMS_EPS).astype(BF16)

    xn = xn_ref[...]
    a = jnp.dot(xn, wa_ref[...], preferred_element_type=F32) + ba_ref[...]
    gt = jnp.dot(xn, wg_ref[...], preferred_element_type=F32) + bg_ref[...]
    o_ref[...] = a * jax.nn.sigmoid(gt)


def norm_glu(x, g, w_pw1, b_pw1, *, tm, tn):
    T, D = x.shape
    nj = D // tn
    b2 = b_pw1.reshape(1, 2 * D)
    return pl.pallas_call(
        _glu_kernel,
        grid=(T // tm, nj),
        in_specs=[
            pl.BlockSpec((tm, D), lambda i, j: (i, 0)),
            pl.BlockSpec((1, D), lambda i, j: (0, 0)),
            pl.BlockSpec((D, tn), lambda i, j: (0, j)),
            pl.BlockSpec((D, tn), lambda i, j: (0, nj + j)),
            pl.BlockSpec((1, tn), lambda i, j: (0, j)),
            pl.BlockSpec((1, tn), lambda i, j: (0, nj + j)),
        ],
        out_specs=pl.BlockSpec((tm, tn), lambda i, j: (i, j)),
        out_shape=jax.ShapeDtypeStruct((T, D), F32),
        scratch_shapes=[pltpu.VMEM((tm, D), BF16)],
        compiler_params=_params(("parallel", "arbitrary")),
        name="glu",
    )(x, g.reshape(1, D), w_pw1, w_pw1, b2, b2)


HALO = 16


def _conv_kernel(x_ref, u_ref, up_ref, un_ref, wdw_ref, bdw_ref, lg_ref, lb_ref, w2_ref, b2_ref,
                 o_ref, pad_ref, y_ref, xs_ref, *, tm, tc, tr, tiles_per_seq):
    i = pl.program_id(0)
    first = (i % tiles_per_seq) == 0
    last = (i % tiles_per_seq) == tiles_per_seq - 1
    pad_ref[0:HALO, :] = jnp.where(first, 0.0, up_ref[...])
    pad_ref[HALO:HALO + tm, :] = u_ref[...]
    pad_ref[HALO + tm:, :] = jnp.where(last, 0.0, un_ref[...])

    D = u_ref.shape[1]
    off = HALO - CONV_PAD

    span = xs_ref.shape[1] - tr
    n_a = span // SUBLANES + 1

    def cols(c, carry):
        c0 = pl.multiple_of(c * tc, tc)
        w = wdw_ref[:, pl.ds(c0, tc)]
        for rb in range(tm // tr):
            acc = jnp.zeros((tr, tc), F32)
            for b in range(SUBLANES):
                xs_ref[b] = pad_ref[pl.ds(rb * tr + b, tr + span), pl.ds(c0, tc)]
                for a in range(n_a):
                    t = SUBLANES * a + b - off
                    if 0 <= t < CONV_WIDTH:
                        acc = acc + xs_ref[b, SUBLANES * a:SUBLANES * a + tr, :] * w[t:t + 1, :]
            y_ref[pl.ds(rb * tr, tr), pl.ds(c0, tc)] = acc
        return carry

    lax.fori_loop(0, D // tc, cols, 0)

    y = y_ref[...] + bdw_ref[...]
    mu = jnp.mean(y, axis=-1, keepdims=True)
    yc = y - mu
    var = jnp.mean(yc * yc, axis=-1, keepdims=True)
    z = yc * lax.rsqrt(var + LN_EPS) * lg_ref[...] + lb_ref[...]
    z = (z * jax.nn.sigmoid(z)).astype(BF16)
    o_ref[...] = x_ref[...] + jnp.dot(z, w2_ref[...], preferred_element_type=F32) + b2_ref[...]


def conv_residual(x, u, w_dw, b_dw, ln_g, ln_b, w_pw2, b_pw2, *, n_seq, tm=512, tc=128, tr=128):
    T, D = x.shape
    S = T // n_seq
    tiles_per_seq = S // tm
    hb = tm // HALO
    n_hb = T // HALO
    kern = functools.partial(_conv_kernel, tm=tm, tc=tc, tr=tr, tiles_per_seq=tiles_per_seq)
    tap_span = (HALO - CONV_PAD + CONV_WIDTH - 1) // SUBLANES * SUBLANES
    row = lambda v: v.reshape(1, D)
    return pl.pallas_call(
        kern,
        grid=(T // tm,),
        in_specs=[
            pl.BlockSpec((tm, D), lambda i: (i, 0)),
            pl.BlockSpec((tm, D), lambda i: (i, 0)),
            pl.BlockSpec((HALO, D), lambda i: (jnp.maximum(i * hb - 1, 0), 0)),
            pl.BlockSpec((HALO, D), lambda i: (jnp.minimum((i + 1) * hb, n_hb - 1), 0)),
            pl.BlockSpec((CONV_WIDTH, D), lambda i: (0, 0)),
            pl.BlockSpec((1, D), lambda i: (0, 0)),
            pl.BlockSpec((1, D), lambda i: (0, 0)),
            pl.BlockSpec((1, D), lambda i: (0, 0)),
            pl.BlockSpec((D, D), lambda i: (0, 0)),
            pl.BlockSpec((1, D), lambda i: (0, 0)),
        ],
        out_specs=pl.BlockSpec((tm, D), lambda i: (i, 0)),
        out_shape=jax.ShapeDtypeStruct((T, D), F32),
        scratch_shapes=[
            pltpu.VMEM((tm + 2 * HALO, D), F32),
            pltpu.VMEM((tm, D), F32),
            pltpu.VMEM((SUBLANES, tr + tap_span, tc), F32),
        ],
        compiler_params=_params(("parallel",)),
        name="conv",
    )(x, u, u, u, w_dw, row(b_dw), row(ln_g), row(ln_b), w_pw2, row(b_pw2))


def _lambda_init_for(layer_idx):
    return 0.8 - 0.6 * math.exp(-0.3 * layer_idx)


def _head_major(w):
    D = w.shape[0]
    return w.reshape(D, 2, N_HEADS, HEAD_DIM).transpose(0, 2, 1, 3).reshape(D, 2 * N_HEADS * HEAD_DIM)


def kernel(x_prompt, x_sample, attn_norm_g, w_qkv, lam_q1, lam_k1, lam_q2, lam_k2, subln_g, w_o,
           conv_norm_g, conv_w_pw1, conv_b_pw1, conv_w_dw, conv_b_dw, conv_ln_g, conv_ln_b,
           conv_w_pw2, conv_b_pw2, mlp_norm_g, w_up, w_down, final_norm_g):
    D = D_MODEL
    n_prompt = x_prompt.shape[0]
    n_seq = n_prompt + x_sample.shape[0]
    xa = x_prompt.reshape(-1, D)
    xb = x_sample.reshape(-1, D)

    wq, wk, wv = jnp.split(w_qkv[0], 3, axis=-1)
    w_qkv_b = jnp.concatenate([_head_major(wq), _head_major(wk), wv], axis=-1).astype(BF16)
    lam_vecs = jnp.stack([lam_q1[0], lam_k1[0], lam_q2[0], lam_k2[0]]).astype(F32)
    qkv = norm_matmul(xa, xb, attn_norm_g[0], w_qkv_b, tm=1024, tn=1024)
    o = diff_attention(qkv, lam_vecs, subln_g[0], _lambda_init_for(0), n_seq=n_seq)
    x = proj_residual(xa, xb, o, w_o[0].astype(BF16), tm=512)
    w_up_b = w_up.astype(BF16)
    w_down_b = w_down.astype(BF16)
    x = mlp_residual(x, mlp_norm_g[0], w_up_b, w_down_b, final_norm_g,
                     layer=0, final_norm=False, tm=1024, tf=1024)

    u = norm_glu(x, conv_norm_g[0], conv_w_pw1[0].astype(BF16), conv_b_pw1[0], tm=1024, tn=1024)
    x = conv_residual(x, u, conv_w_dw[0], conv_b_dw[0], conv_ln_g[0], conv_ln_b[0],
                      conv_w_pw2[0].astype(BF16), conv_b_pw2[0], n_seq=n_seq)
    last = functools.partial(mlp_residual, x, mlp_norm_g[1], w_up_b, w_down_b, final_norm_g,
                             layer=1, final_norm=True, tm=1024, tf=1024)
    y_prompt = last(row_start=0, n_rows=xa.shape[0])
    y_sample = last(row_start=xa.shape[0], n_rows=xb.shape[0])
    return (y_prompt.reshape(x_prompt.shape), y_sample.reshape(x_sample.shape))
```

```python
import functools
import math

import jax
import jax.numpy as jnp
from jax import lax
from jax.experimental import pallas as pl
from jax.experimental.pallas import tpu as pltpu

D_MODEL = 2048
SEQ = 4096
HEAD_DIM = 64
N_HEADS = D_MODEL // (2 * HEAD_DIM)
V_DIM = 2 * HEAD_DIM
CONV_WIDTH = 31
CONV_PAD = (CONV_WIDTH - 1) // 2
D_FF = 4 * D_MODEL
RMS_EPS = 1e-6
SUBLN_EPS = 1e-5
LN_EPS = 1e-5

LANES = 128
SUBLANES = 8
VMEM_LIMIT = 56 * 1024 * 1024

F32 = jnp.float32
BF16 = jnp.bfloat16


ATTN_VMEM_LIMIT = 62 * 1024 * 1024
MLP_VMEM_LIMIT = 62 * 1024 * 1024


def _params(semantics, vmem_limit=VMEM_LIMIT):
    return pltpu.CompilerParams(dimension_semantics=semantics, vmem_limit_bytes=vmem_limit)


def _rmsnorm_rows(x, g, eps):
    return x * lax.rsqrt(jnp.mean(x * x, axis=-1, keepdims=True) + eps) * g


def _split_rows_specs(tm, D, n_a, rank):
    if rank == 1:
        return [pl.BlockSpec((tm, D), lambda i: (jnp.minimum(i, n_a - 1), 0)),
                pl.BlockSpec((tm, D), lambda i: (jnp.maximum(i - n_a, 0), 0))]
    return [pl.BlockSpec((tm, D), lambda i, j: (jnp.minimum(i, n_a - 1), 0)),
            pl.BlockSpec((tm, D), lambda i, j: (jnp.maximum(i - n_a, 0), 0))]


def _norm_matmul_kernel(xa_ref, xb_ref, g_ref, w_ref, o_ref, xn_ref, *, n_a):
    i = pl.program_id(0)
    first = pl.program_id(1) == 0

    @pl.when(first & (i < n_a))
    def _():
        xn_ref[...] = _rmsnorm_rows(xa_ref[...], g_ref[...], RMS_EPS).astype(BF16)

    @pl.when(first & (i >= n_a))
    def _():
        xn_ref[...] = _rmsnorm_rows(xb_ref[...], g_ref[...], RMS_EPS).astype(BF16)

    o_ref[...] = jnp.dot(xn_ref[...], w_ref[...], preferred_element_type=F32).astype(o_ref.dtype)


def norm_matmul(xa, xb, g, w, *, tm, tn):
    D = xa.shape[1]
    T = xa.shape[0] + xb.shape[0]
    N = w.shape[1]
    n_a = xa.shape[0] // tm
    return pl.pallas_call(
        functools.partial(_norm_matmul_kernel, n_a=n_a),
        grid=(T // tm, N // tn),
        in_specs=_split_rows_specs(tm, D, n_a, 2) + [
            pl.BlockSpec((1, D), lambda i, j: (0, 0)),
            pl.BlockSpec((D, tn), lambda i, j: (0, j)),
        ],
        out_specs=pl.BlockSpec((tm, tn), lambda i, j: (i, j)),
        out_shape=jax.ShapeDtypeStruct((T, N), BF16),
        scratch_shapes=[pltpu.VMEM((tm, D), BF16)],
        compiler_params=_params(("parallel", "arbitrary")),
        name="qkv",
    )(xa, xb, g.reshape(1, D), w)


def _attn_kernel(lam_ref, g_ref, q0_ref, q1_ref, k0_ref, k1_ref, v_ref, o_ref,
                 bias_ref, vaug_ref, s0_ref, s1_ref, m0_ref, m1_ref, a0_ref, a1_ref,
                 *, tq, tb, nq, n_tiles, lambda_init):
    S = k0_ref.shape[0]
    g = pl.program_id(0)
    ga = jnp.minimum(g, n_tiles - 1)
    gb = jnp.clip(g - 1, 0, n_tiles - 1)
    qi = ga % nq
    h = (ga // nq) % N_HEADS

    @pl.when(g == 0)
    def _():
        s1_ref[...] = jnp.zeros_like(s1_ref)
        m1_ref[...] = jnp.zeros_like(m1_ref)
        a1_ref[...] = jnp.ones_like(a1_ref)
        r = lax.broadcasted_iota(jnp.int32, (tb, 2 * S), 0)
        j = lax.broadcasted_iota(jnp.int32, (tb, 2 * S), 1)
        bias_ref[...] = jnp.abs(r - j + S).astype(F32)

    neg_slope = -jnp.exp2(jnp.full((1, 1), -0.5, F32) * (h + 1).astype(F32))

    @pl.when(gb % nq == 0)
    def _():
        lane = lax.broadcasted_iota(jnp.int32, (S, LANES), 1)
        vaug_ref[:, :V_DIM] = v_ref[...]
        vaug_ref[:, V_DIM:] = jnp.where(lane == 0, 1.0, 0.0).astype(BF16)

    lv = lam_ref[...]
    lam = (jnp.exp(jnp.sum(lv[0:1] * lv[1:2], axis=-1, keepdims=True))
           - jnp.exp(jnp.sum(lv[2:3] * lv[3:4], axis=-1, keepdims=True)) + lambda_init)
    b0 = pl.multiple_of(S - qi * tq, LANES)

    def step(s_w, m_w, a_w, s_r, m_r, a_r):
        lane = lax.broadcasted_iota(jnp.int32, (tq, LANES), 1)
        mine = (lane >= HEAD_DIM).astype(jnp.int32) == h % 2
        scale = jnp.asarray(HEAD_DIM ** -0.5, BF16)
        zero = jnp.zeros((tq, LANES), BF16)
        qa = jnp.where(mine, q0_ref[...], zero) * scale
        qb = jnp.where(mine, q1_ref[...], zero) * scale
        qq = jnp.concatenate([jnp.concatenate([qa, zero], axis=1),
                              jnp.concatenate([zero, qb], axis=1)], axis=0)
        kk = jnp.concatenate([k0_ref[...], k1_ref[...]], axis=1)
        s = lax.dot_general(qq, kk, (((1,), (1,)), ((), ())), preferred_element_type=F32)
        bias = jnp.concatenate(
            [bias_ref[:, pl.ds(pl.multiple_of(b0 - r0, LANES), S)] for r0 in range(0, tq, tb)],
            axis=0) * neg_slope
        s = s + jnp.concatenate([bias, bias], axis=0)
        s_w[...] = s
        mp = s[:, :LANES]
        for c in range(1, S // LANES):
            mp = jnp.maximum(mp, s[:, c * LANES:(c + 1) * LANES])
        m_w[...] = mp

        m = jnp.max(m_r[...], axis=-1, keepdims=True)
        p = jnp.exp(s_r[...] - m).astype(BF16)
        a_w[...] = jnp.dot(p, vaug_ref[...], preferred_element_type=F32)

        acc = a_r[...]
        o0 = acc[:tq, :V_DIM] / acc[:tq, V_DIM:V_DIM + 1]
        o1 = acc[tq:, :V_DIM] / acc[tq:, V_DIM:V_DIM + 1]
        o = o0 - lam * o1
        o = _rmsnorm_rows(o, g_ref[...], SUBLN_EPS) * (1.0 - lambda_init)
        o_ref[...] = o.astype(o_ref.dtype)

    @pl.when(g % 2 == 0)
    def _():
        step(s0_ref, m0_ref, a0_ref, s1_ref, m1_ref, a1_ref)

    @pl.when(g % 2 == 1)
    def _():
        step(s1_ref, m1_ref, a1_ref, s0_ref, m0_ref, a0_ref)


def diff_attention(qkv, lam_vecs, subln_g, lambda_init, *, n_seq, tq=512, tb=256):
    T = qkv.shape[0]
    S = T // n_seq
    nq = S // tq
    H = N_HEADS
    n_tiles = n_seq * H * nq
    kern = functools.partial(_attn_kernel, tq=tq, tb=tb, nq=nq, n_tiles=n_tiles, lambda_init=lambda_init)

    def tile(t):
        return t // (nq * H), (t // nq) % H, t % nq

    def q_map(m):
        def index(g):
            b, h, i = tile(jnp.minimum(g, n_tiles - 1))
            return (b * nq + i, m * (H // 2) + h // 2)
        return index

    def k_map(m):
        def index(g):
            b, h, _ = tile(jnp.minimum(g, n_tiles - 1))
            return (b, H + m * (H // 2) + h // 2)
        return index

    def v_map(g):
        b, h, _ = tile(jnp.clip(g - 1, 0, n_tiles - 1))
        return (b, 2 * H + h)

    def o_map(g):
        b, h, i = tile(jnp.maximum(g - 2, 0))
        return (b * nq + i, h)

    return pl.pallas_call(
        kern,
        grid=(n_tiles + 2,),
        in_specs=[
            pl.BlockSpec((4, HEAD_DIM), lambda g: (0, 0)),
            pl.BlockSpec((1, V_DIM), lambda g: (0, 0)),
            pl.BlockSpec((tq, LANES), q_map(0)),
            pl.BlockSpec((tq, LANES), q_map(1)),
            pl.BlockSpec((S, LANES), k_map(0)),
            pl.BlockSpec((S, LANES), k_map(1)),
            pl.BlockSpec((S, LANES), v_map),
        ],
        out_specs=pl.BlockSpec((tq, V_DIM), o_map),
        out_shape=jax.ShapeDtypeStruct((T, D_MODEL), BF16),
        scratch_shapes=[
            pltpu.VMEM((tb, 2 * S), F32),
            pltpu.VMEM((S, 2 * LANES), BF16),
            pltpu.VMEM((2 * tq, S), F32),
            pltpu.VMEM((2 * tq, S), F32),
            pltpu.VMEM((2 * tq, LANES), F32),
            pltpu.VMEM((2 * tq, LANES), F32),
            pltpu.VMEM((2 * tq, 2 * LANES), F32),
            pltpu.VMEM((2 * tq, 2 * LANES), F32),
        ],
        compiler_params=_params(("arbitrary",), ATTN_VMEM_LIMIT),
        name="attn",
    )(lam_vecs, subln_g.reshape(1, V_DIM), qkv, qkv, qkv, qkv, qkv)


def _proj_kernel(xa_ref, xb_ref, a_ref, w_ref, o_ref, *, n_a):
    x = jnp.where(pl.program_id(0) < n_a, xa_ref[...], xb_ref[...])
    o_ref[...] = x + jnp.dot(a_ref[...], w_ref[...], preferred_element_type=F32)


def proj_residual(xa, xb, a, w, *, tm):
    T, D = a.shape
    n_a = xa.shape[0] // tm
    return pl.pallas_call(
        functools.partial(_proj_kernel, n_a=n_a),
        grid=(T // tm,),
        in_specs=_split_rows_specs(tm, D, n_a, 1) + [
            pl.BlockSpec((tm, D), lambda i: (i, 0)),
            pl.BlockSpec((D, D), lambda i: (0, 0)),
        ],
        out_specs=pl.BlockSpec((tm, D), lambda i: (i, 0)),
        out_shape=jax.ShapeDtypeStruct((T, D), F32),
        compiler_params=_params(("parallel",)),
        name="proj",
    )(xa, xb, a, w)


def _mlp_kernel(x_ref, g_ref, wu_ref, wd_ref, gf_ref, o_ref, xn_ref, *, final_norm):
    j = pl.program_id(1)

    @pl.when(j == 0)
    def _():
        x = x_ref[...]
        xn_ref[...] = _rmsnorm_rows(x, g_ref[...], RMS_EPS).astype(BF16)
        o_ref[...] = x

    hdn = jnp.dot(xn_ref[...], wu_ref[...], preferred_element_type=F32)
    hdn = jnp.maximum(hdn, 0.0)
    hdn = (hdn * hdn).astype(BF16)
    o_ref[...] += jnp.dot(hdn, wd_ref[...], preferred_element_type=F32)

    if final_norm:
        @pl.when(j == pl.num_programs(1) - 1)
        def _():
            o_ref[...] = _rmsnorm_rows(o_ref[...], gf_ref[...], RMS_EPS)


def mlp_residual(x, g, w_up, w_down, g_final, *, layer, final_norm, tm, tf, row_start=0, n_rows=None):
    D = x.shape[1]
    n_rows = x.shape[0] if n_rows is None else n_rows
    F = w_up.shape[2]
    i0 = row_start // tm
    kern = functools.partial(_mlp_kernel, final_norm=final_norm)
    return pl.pallas_call(
        kern,
        grid=(n_rows // tm, F // tf),
        in_specs=[
            pl.BlockSpec((tm, D), lambda i, j: (i0 + i, 0)),
            pl.BlockSpec((1, D), lambda i, j: (0, 0)),
            pl.BlockSpec((None, D, tf), lambda i, j: (layer, 0, j)),
            pl.BlockSpec((None, tf, D), lambda i, j: (layer, j, 0)),
            pl.BlockSpec((1, D), lambda i, j: (0, 0)),
        ],
        out_specs=pl.BlockSpec((tm, D), lambda i, j: (i, 0)),
        out_shape=jax.ShapeDtypeStruct((n_rows, D), F32),
        scratch_shapes=[pltpu.VMEM((tm, D), BF16)],
        compiler_params=_params(("parallel", "arbitrary"), MLP_VMEM_LIMIT),
        name="mlp",
    )(x, g.reshape(1, D), w_up, w_down, g_final.reshape(1, D))


def _glu_kernel(x_ref, g_ref, wa_ref, wg_ref, ba_ref, bg_ref, o_ref, xn_ref):
    @pl.when(pl.program_id(1) == 0)
    def _():
        xn_ref[...] = _rmsnorm_rows(x_ref[...], g_ref[...], RMS_EPS).astype(BF16)

    xn = xn_ref[...]
    a = jnp.dot(xn, wa_ref[...], preferred_element_type=F32) + ba_ref[...]
    gt = jnp.dot(xn, wg_ref[...], preferred_element_type=F32) + bg_ref[...]
    o_ref[...] = a * jax.nn.sigmoid(gt)


def norm_glu(x, g, w_pw1, b_pw1, *, tm, tn):
    T, D = x.shape
    nj = D // tn
    b2 = b_pw1.reshape(1, 2 * D)
    return pl.pallas_call(
        _glu_kernel,
        grid=(T // tm, nj),
        in_specs=[
            pl.BlockSpec((tm, D), lambda i, j: (i, 0)),
            pl.BlockSpec((1, D), lambda i, j: (0, 0)),
            pl.BlockSpec((D, tn), lambda i, j: (0, j)),
            pl.BlockSpec((D, tn), lambda i, j: (0, nj + j)),
            pl.BlockSpec((1, tn), lambda i, j: (0, j)),
            pl.BlockSpec((1, tn), lambda i, j: (0, nj + j)),
        ],
        out_specs=pl.BlockSpec((tm, tn), lambda i, j: (i, j)),
        out_shape=jax.ShapeDtypeStruct((T, D), F32),
        scratch_shapes=[pltpu.VMEM((tm, D), BF16)],
        compiler_params=_params(("parallel", "arbitrary")),
        name="glu",
    )(x, g.reshape(1, D), w_pw1, w_pw1, b2, b2)


HALO = 16


def _conv_kernel(x_ref, u_ref, up_ref, un_ref, wdw_ref, bdw_ref, lg_ref, lb_ref, w2_ref, b2_ref,
                 o_ref, pad_ref, y_ref, xs_ref, *, tm, tc, tr, tiles_per_seq):
    i = pl.program_id(0)
    first = (i % tiles_per_seq) == 0
    last = (i % tiles_per_seq) == tiles_per_seq - 1
    pad_ref[0:HALO, :] = jnp.where(first, 0.0, up_ref[...])
    pad_ref[HALO:HALO + tm, :] = u_ref[...]
    pad_ref[HALO + tm:, :] = jnp.where(last, 0.0, un_ref[...])

    D = u_ref.shape[1]
    off = HALO - CONV_PAD

    span = xs_ref.shape[1] - tr
    n_a = span // SUBLANES + 1

    def cols(c, carry):
        c0 = pl.multiple_of(c * tc, tc)
        w = wdw_ref[:, pl.ds(c0, tc)]
        for rb in range(tm // tr):
            acc = jnp.zeros((tr, tc), F32)
            for b in range(SUBLANES):
                xs_ref[b] = pad_ref[pl.ds(rb * tr + b, tr + span), pl.ds(c0, tc)]
                for a in range(n_a):
                    t = SUBLANES * a + b - off
                    if 0 <= t < CONV_WIDTH:
                        acc = acc + xs_ref[b, SUBLANES * a:SUBLANES * a + tr, :] * w[t:t + 1, :]
            y_ref[pl.ds(rb * tr, tr), pl.ds(c0, tc)] = acc
        return carry

    lax.fori_loop(0, D // tc, cols, 0)

    y = y_ref[...] + bdw_ref[...]
    mu = jnp.mean(y, axis=-1, keepdims=True)
    yc = y - mu
    var = jnp.mean(yc * yc, axis=-1, keepdims=True)
    z = yc * lax.rsqrt(var + LN_EPS) * lg_ref[...] + lb_ref[...]
    z = (z * jax.nn.sigmoid(z)).astype(BF16)
    o_ref[...] = x_ref[...] + jnp.dot(z, w2_ref[...], preferred_element_type=F32) + b2_ref[...]


def conv_residual(x, u, w_dw, b_dw, ln_g, ln_b, w_pw2, b_pw2, *, n_seq, tm=512, tc=128, tr=128):
    T, D = x.shape
    S = T // n_seq
    tiles_per_seq = S // tm
    hb = tm // HALO
    n_hb = T // HALO
    kern = functools.partial(_conv_kernel, tm=tm, tc=tc, tr=tr, tiles_per_seq=tiles_per_seq)
    tap_span = (HALO - CONV_PAD + CONV_WIDTH - 1) // SUBLANES * SUBLANES
    row = lambda v: v.reshape(1, D)
    return pl.pallas_call(
        kern,
        grid=(T // tm,),
        in_specs=[
            pl.BlockSpec((tm, D), lambda i: (i, 0)),
            pl.BlockSpec((tm, D), lambda i: (i, 0)),
            pl.BlockSpec((HALO, D), lambda i: (jnp.maximum(i * hb - 1, 0), 0)),
            pl.BlockSpec((HALO, D), lambda i: (jnp.minimum((i + 1) * hb, n_hb - 1), 0)),
            pl.BlockSpec((CONV_WIDTH, D), lambda i: (0, 0)),
            pl.BlockSpec((1, D), lambda i: (0, 0)),
            pl.BlockSpec((1, D), lambda i: (0, 0)),
            pl.BlockSpec((1, D), lambda i: (0, 0)),
            pl.BlockSpec((D, D), lambda i: (0, 0)),
            pl.BlockSpec((1, D), lambda i: (0, 0)),
        ],
        out_specs=pl.BlockSpec((tm, D), lambda i: (i, 0)),
        out_shape=jax.ShapeDtypeStruct((T, D), F32),
        scratch_shapes=[
            pltpu.VMEM((tm + 2 * HALO, D), F32),
            pltpu.VMEM((tm, D), F32),
            pltpu.VMEM((SUBLANES, tr + tap_span, tc), F32),
        ],
        compiler_params=_params(("parallel",)),
        name="conv",
    )(x, u, u, u, w_dw, row(b_dw), row(ln_g), row(ln_b), w_pw2, row(b_pw2))


def _lambda_init_for(layer_idx):
    return 0.8 - 0.6 * math.exp(-0.3 * layer_idx)


def kernel(x_prompt, x_sample, attn_norm_g, w_qkv, lam_q1, lam_k1, lam_q2, lam_k2, subln_g, w_o,
           conv_norm_g, conv_w_pw1, conv_b_pw1, conv_w_dw, conv_b_dw, conv_ln_g, conv_ln_b,
           conv_w_pw2, conv_b_pw2, mlp_norm_g, w_up, w_down, final_norm_g):
    D = D_MODEL
    n_prompt = x_prompt.shape[0]
    n_seq = n_prompt + x_sample.shape[0]
    xa = x_prompt.reshape(-1, D)
    xb = x_sample.reshape(-1, D)

    w_qkv_b = w_qkv[0].astype(BF16)
    lam_vecs = jnp.stack([lam_q1[0], lam_k1[0], lam_q2[0], lam_k2[0]]).astype(F32)
    qkv = norm_matmul(xa, xb, attn_norm_g[0], w_qkv_b, tm=1024, tn=1024)
    o = diff_attention(qkv, lam_vecs, subln_g[0], _lambda_init_for(0), n_seq=n_seq)
    x = proj_residual(xa, xb, o, w_o[0].astype(BF16), tm=512)
    w_up_b = w_up.astype(BF16)
    w_down_b = w_down.astype(BF16)
    x = mlp_residual(x, mlp_norm_g[0], w_up_b, w_down_b, final_norm_g,
                     layer=0, final_norm=False, tm=1024, tf=1024)

    u = norm_glu(x, conv_norm_g[0], conv_w_pw1[0].astype(BF16), conv_b_pw1[0], tm=1024, tn=1024)
    x = conv_residual(x, u, conv_w_dw[0], conv_b_dw[0], conv_ln_g[0], conv_ln_b[0],
                      conv_w_pw2[0].astype(BF16), conv_b_pw2[0], n_seq=n_seq)
    last = functools.partial(mlp_residual, x, mlp_norm_g[1], w_up_b, w_down_b, final_norm_g,
                             layer=1, final_norm=True, tm=1024, tf=1024)
    y_prompt = last(row_start=0, n_rows=xa.shape[0])
    y_sample = last(row_start=xa.shape[0], n_rows=xb.shape[0])
    return (y_prompt.reshape(x_prompt.shape), y_sample.reshape(x_sample.shape))
```

```python
import functools
import math

import jax
import jax.numpy as jnp
from jax import lax
from jax.experimental import pallas as pl
from jax.experimental.pallas import tpu as pltpu

D_MODEL = 2048
SEQ = 4096
HEAD_DIM = 64
N_HEADS = D_MODEL // (2 * HEAD_DIM)
V_DIM = 2 * HEAD_DIM
CONV_WIDTH = 31
CONV_PAD = (CONV_WIDTH - 1) // 2
D_FF = 4 * D_MODEL
RMS_EPS = 1e-6
SUBLN_EPS = 1e-5
LN_EPS = 1e-5

LANES = 128
SUBLANES = 8
VMEM_LIMIT = 56 * 1024 * 1024

F32 = jnp.float32
BF16 = jnp.bfloat16


ATTN_VMEM_LIMIT = 62 * 1024 * 1024
MLP_VMEM_LIMIT = 62 * 1024 * 1024


def _params(semantics, vmem_limit=VMEM_LIMIT):
    return pltpu.CompilerParams(dimension_semantics=semantics, vmem_limit_bytes=vmem_limit)


def _rmsnorm_rows(x, g, eps):
    return x * lax.rsqrt(jnp.mean(x * x, axis=-1, keepdims=True) + eps) * g


def _split_rows_specs(tm, D, n_a, rank):
    if rank == 1:
        return [pl.BlockSpec((tm, D), lambda i: (jnp.minimum(i, n_a - 1), 0)),
                pl.BlockSpec((tm, D), lambda i: (jnp.maximum(i - n_a, 0), 0))]
    return [pl.BlockSpec((tm, D), lambda i, j: (jnp.minimum(i, n_a - 1), 0)),
            pl.BlockSpec((tm, D), lambda i, j: (jnp.maximum(i - n_a, 0), 0))]


def _norm_matmul_kernel(xa_ref, xb_ref, g_ref, w_ref, o_ref, xn_ref, *, n_a):
    i = pl.program_id(0)
    first = pl.program_id(1) == 0

    @pl.when(first & (i < n_a))
    def _():
        xn_ref[...] = _rmsnorm_rows(xa_ref[...], g_ref[...], RMS_EPS).astype(BF16)

    @pl.when(first & (i >= n_a))
    def _():
        xn_ref[...] = _rmsnorm_rows(xb_ref[...], g_ref[...], RMS_EPS).astype(BF16)

    o_ref[...] = jnp.dot(xn_ref[...], w_ref[...], preferred_element_type=F32).astype(o_ref.dtype)


def norm_matmul(xa, xb, g, w, *, tm, tn):
    D = xa.shape[1]
    T = xa.shape[0] + xb.shape[0]
    N = w.shape[1]
    n_a = xa.shape[0] // tm
    return pl.pallas_call(
        functools.partial(_norm_matmul_kernel, n_a=n_a),
        grid=(T // tm, N // tn),
        in_specs=_split_rows_specs(tm, D, n_a, 2) + [
            pl.BlockSpec((1, D), lambda i, j: (0, 0)),
            pl.BlockSpec((D, tn), lambda i, j: (0, j)),
        ],
        out_specs=pl.BlockSpec((tm, tn), lambda i, j: (i, j)),
        out_shape=jax.ShapeDtypeStruct((T, N), BF16),
        scratch_shapes=[pltpu.VMEM((tm, D), BF16)],
        compiler_params=_params(("parallel", "arbitrary")),
        name="qkv",
    )(xa, xb, g.reshape(1, D), w)


def _attn_kernel(lam_ref, g_ref, q0_ref, q1_ref, k0_ref, k1_ref, v_ref, o_ref,
                 bias_ref, vaug_ref, s0_ref, s1_ref, m0_ref, m1_ref, a0_ref, a1_ref,
                 *, tq, tb, nq, n_tiles, lambda_init):
    S = k0_ref.shape[0]
    g = pl.program_id(0)
    ga = jnp.minimum(g, n_tiles - 1)
    gb = jnp.clip(g - 1, 0, n_tiles - 1)
    qi = ga % nq
    h = (ga // nq) % N_HEADS

    @pl.when(g == 0)
    def _():
        s1_ref[...] = jnp.zeros_like(s1_ref)
        m1_ref[...] = jnp.zeros_like(m1_ref)
        a1_ref[...] = jnp.ones_like(a1_ref)
        r = lax.broadcasted_iota(jnp.int32, (tb, 2 * S), 0)
        j = lax.broadcasted_iota(jnp.int32, (tb, 2 * S), 1)
        bias_ref[...] = jnp.abs(r - j + S).astype(F32)

    neg_slope = -jnp.exp2(jnp.full((1, 1), -0.5, F32) * (h + 1).astype(F32))

    @pl.when(gb % nq == 0)
    def _():
        lane = lax.broadcasted_iota(jnp.int32, (S, LANES), 1)
        vaug_ref[:, :V_DIM] = v_ref[...]
        vaug_ref[:, V_DIM:] = jnp.where(lane == 0, 1.0, 0.0).astype(BF16)

    lv = lam_ref[...]
    lam = (jnp.exp(jnp.sum(lv[0:1] * lv[1:2], axis=-1, keepdims=True))
           - jnp.exp(jnp.sum(lv[2:3] * lv[3:4], axis=-1, keepdims=True)) + lambda_init)
    b0 = pl.multiple_of(S - qi * tq, LANES)

    def step(s_w, m_w, a_w, s_r, m_r, a_r):
        lane = lax.broadcasted_iota(jnp.int32, (tq, LANES), 1)
        mine = (lane >= HEAD_DIM).astype(jnp.int32) == h % 2
        scale = jnp.asarray(HEAD_DIM ** -0.5, BF16)
        zero = jnp.zeros((tq, LANES), BF16)
        qa = jnp.where(mine, q0_ref[...], zero) * scale
        qb = jnp.where(mine, q1_ref[...], zero) * scale
        qq = jnp.concatenate([jnp.concatenate([qa, zero], axis=1),
                              jnp.concatenate([zero, qb], axis=1)], axis=0)
        kk = jnp.concatenate([k0_ref[...], k1_ref[...]], axis=1)
        s = lax.dot_general(qq, kk, (((1,), (1,)), ((), ())), preferred_element_type=F32)
        bias = jnp.concatenate(
            [bias_ref[:, pl.ds(pl.multiple_of(b0 - r0, LANES), S)] for r0 in range(0, tq, tb)],
            axis=0) * neg_slope
        s = s + jnp.concatenate([bias, bias], axis=0)
        s_w[...] = s
        mp = s[:, :LANES]
        for c in range(1, S // LANES):
            mp = jnp.maximum(mp, s[:, c * LANES:(c + 1) * LANES])
        m_w[...] = mp

        m = jnp.max(m_r[...], axis=-1, keepdims=True)
        p = jnp.exp(s_r[...] - m).astype(BF16)
        a_w[...] = jnp.dot(p, vaug_ref[...], preferred_element_type=F32)

        acc = a_r[...]
        o0 = acc[:tq, :V_DIM] / acc[:tq, V_DIM:V_DIM + 1]
        o1 = acc[tq:, :V_DIM] / acc[tq:, V_DIM:V_DIM + 1]
        o = o0 - lam * o1
        o = _rmsnorm_rows(o, g_ref[...], SUBLN_EPS) * (1.0 - lambda_init)
        o_ref[...] = o.astype(o_ref.dtype)

    @pl.when(g % 2 == 0)
    def _():
        step(s0_ref, m0_ref, a0_ref, s1_ref, m1_ref, a1_ref)

    @pl.when(g % 2 == 1)
    def _():
        step(s1_ref, m1_ref, a1_ref, s0_ref, m0_ref, a0_ref)


def diff_attention(qkv, lam_vecs, subln_g, lambda_init, *, n_seq, tq=512, tb=256):
    T = qkv.shape[0]
    S = T // n_seq
    nq = S // tq
    H = N_HEADS
    n_tiles = n_seq * H * nq
    kern = functools.partial(_attn_kernel, tq=tq, tb=tb, nq=nq, n_tiles=n_tiles, lambda_init=lambda_init)

    def tile(t):
        return t // (nq * H), (t // nq) % H, t % nq

    def q_map(m):
        def index(g):
            b, h, i = tile(jnp.minimum(g, n_tiles - 1))
            return (b * nq + i, m * (H // 2) + h // 2)
        return index

    def k_map(m):
        def index(g):
            b, h, _ = tile(jnp.minimum(g, n_tiles - 1))
            return (b, H + m * (H // 2) + h // 2)
        return index

    def v_map(g):
        b, h, _ = tile(jnp.clip(g - 1, 0, n_tiles - 1))
        return (b, 2 * H + h)

    def o_map(g):
        b, h, i = tile(jnp.maximum(g - 2, 0))
        return (b * nq + i, h)

    return pl.pallas_call(
        kern,
        grid=(n_tiles + 2,),
        in_specs=[
            pl.BlockSpec((4, HEAD_DIM), lambda g: (0, 0)),
            pl.BlockSpec((1, V_DIM), lambda g: (0, 0)),
            pl.BlockSpec((tq, LANES), q_map(0)),
            pl.BlockSpec((tq, LANES), q_map(1)),
            pl.BlockSpec((S, LANES), k_map(0)),
            pl.BlockSpec((S, LANES), k_map(1)),
            pl.BlockSpec((S, LANES), v_map),
        ],
        out_specs=pl.BlockSpec((tq, V_DIM), o_map),
        out_shape=jax.ShapeDtypeStruct((T, D_MODEL), BF16),
        scratch_shapes=[
            pltpu.VMEM((tb, 2 * S), F32),
            pltpu.VMEM((S, 2 * LANES), BF16),
            pltpu.VMEM((2 * tq, S), F32),
            pltpu.VMEM((2 * tq, S), F32),
            pltpu.VMEM((2 * tq, LANES), F32),
            pltpu.VMEM((2 * tq, LANES), F32),
            pltpu.VMEM((2 * tq, 2 * LANES), F32),
            pltpu.VMEM((2 * tq, 2 * LANES), F32),
        ],
        compiler_params=_params(("arbitrary",), ATTN_VMEM_LIMIT),
        name="attn",
    )(lam_vecs, subln_g.reshape(1, V_DIM), qkv, qkv, qkv, qkv, qkv)


def _proj_kernel(xa_ref, xb_ref, a_ref, w_ref, o_ref, *, n_a):
    x = jnp.where(pl.program_id(0) < n_a, xa_ref[...], xb_ref[...])
    o_ref[...] = x + jnp.dot(a_ref[...], w_ref[...], preferred_element_type=F32)


def proj_residual(xa, xb, a, w, *, tm):
    T, D = a.shape
    n_a = xa.shape[0] // tm
    return pl.pallas_call(
        functools.partial(_proj_kernel, n_a=n_a),
        grid=(T // tm,),
        in_specs=_split_rows_specs(tm, D, n_a, 1) + [
            pl.BlockSpec((tm, D), lambda i: (i, 0)),
            pl.BlockSpec((D, D), lambda i: (0, 0)),
        ],
        out_specs=pl.BlockSpec((tm, D), lambda i: (i, 0)),
        out_shape=jax.ShapeDtypeStruct((T, D), F32),
        compiler_params=_params(("parallel",)),
        name="proj",
    )(xa, xb, a, w)


def _mlp_kernel(x_ref, g_ref, wu_ref, wd_ref, gf_ref, o_ref, xn_ref, *, final_norm):
    j = pl.program_id(1)

    @pl.when(j == 0)
    def _():
        x = x_ref[...]
        xn_ref[...] = _rmsnorm_rows(x, g_ref[...], RMS_EPS).astype(BF16)
        o_ref[...] = x

    hdn = jnp.dot(xn_ref[...], wu_ref[...], preferred_element_type=F32)
    hdn = jnp.maximum(hdn, 0.0)
    hdn = (hdn * hdn).astype(BF16)
    o_ref[...] += jnp.dot(hdn, wd_ref[...], preferred_element_type=F32)

    if final_norm:
        @pl.when(j == pl.num_programs(1) - 1)
        def _():
            o_ref[...] = _rmsnorm_rows(o_ref[...], gf_ref[...], RMS_EPS)


def mlp_residual(x, g, w_up, w_down, g_final, *, layer, final_norm, tm, tf, row_start=0, n_rows=None):
    D = x.shape[1]
    n_rows = x.shape[0] if n_rows is None else n_rows
    F = w_up.shape[2]
    i0 = row_start // tm
    kern = functools.partial(_mlp_kernel, final_norm=final_norm)
    return pl.pallas_call(
        kern,
        grid=(n_rows // tm, F // tf),
        in_specs=[
            pl.BlockSpec((tm, D), lambda i, j: (i0 + i, 0)),
            pl.BlockSpec((1, D), lambda i, j: (0, 0)),
            pl.BlockSpec((None, D, tf), lambda i, j: (layer, 0, j)),
            pl.BlockSpec((None, tf, D), lambda i, j: (layer, j, 0)),
            pl.BlockSpec((1, D), lambda i, j: (0, 0)),
        ],
        out_specs=pl.BlockSpec((tm, D), lambda i, j: (i, 0)),
        out_shape=jax.ShapeDtypeStruct((n_rows, D), F32),
        scratch_shapes=[pltpu.VMEM((tm, D), BF16)],
        compiler_params=_params(("parallel", "arbitrary"), MLP_VMEM_LIMIT),
        name="mlp",
    )(x, g.reshape(1, D), w_up, w_down, g_final.reshape(1, D))


def _glu_kernel(x_ref, g_ref, wa_ref, wg_ref, ba_ref, bg_ref, o_ref, xn_ref):
    @pl.when(pl.program_id(1) == 0)
    def _():
        xn_ref[...] = _rmsnorm_rows(x_ref[...], g_ref[...], RMS_EPS).astype(BF16)

    xn = xn_ref[...]
    a = jnp.dot(xn, wa_ref[...], preferred_element_type=F32) + ba_ref[...]
    gt = jnp.dot(xn, wg_ref[...], preferred_element_type=F32) + bg_ref[...]
    o_ref[...] = a * jax.nn.sigmoid(gt)


def norm_glu(x, g, w_pw1, b_pw1, *, tm, tn):
    T, D = x.shape
    nj = D // tn
    b2 = b_pw1.reshape(1, 2 * D)
    return pl.pallas_call(
        _glu_kernel,
        grid=(T // tm, nj),
        in_specs=[
            pl.BlockSpec((tm, D), lambda i, j: (i, 0)),
            pl.BlockSpec((1, D), lambda i, j: (0, 0)),
            pl.BlockSpec((D, tn), lambda i, j: (0, j)),
            pl.BlockSpec((D, tn), lambda i, j: (0, nj + j)),
            pl.BlockSpec((1, tn), lambda i, j: (0, j)),
            pl.BlockSpec((1, tn), lambda i, j: (0, nj + j)),
        ],
        out_specs=pl.BlockSpec((tm, tn), lambda i, j: (i, j)),
        out_shape=jax.ShapeDtypeStruct((T, D), F32),
        scratch_shapes=[pltpu.VMEM((tm, D), BF16)],
        compiler_params=_params(("parallel", "arbitrary")),
        name="glu",
    )(x, g.reshape(1, D), w_pw1, w_pw1, b2, b2)


HALO = 16


def _conv_kernel(x_ref, u_ref, up_ref, un_ref, wdw_ref, bdw_ref, lg_ref, lb_ref, w2_ref, b2_ref,
                 o_ref, pad_ref, y_ref, xs_ref, *, tm, tc, tr, tiles_per_seq):
    i = pl.program_id(0)
    first = (i % tiles_per_seq) == 0
    last = (i % tiles_per_seq) == tiles_per_seq - 1
    pad_ref[0:HALO, :] = jnp.where(first, 0.0, up_ref[...])
    pad_ref[HALO:HALO + tm, :] = u_ref[...]
    pad_ref[HALO + tm:, :] = jnp.where(last, 0.0, un_ref[...])

    D = u_ref.shape[1]
    off = HALO - CONV_PAD

    span = xs_ref.shape[1] - tr
    n_a = span // SUBLANES + 1

    def cols(c, carry):
        c0 = pl.multiple_of(c * tc, tc)
        w = wdw_ref[:, pl.ds(c0, tc)]
        for rb in range(tm // tr):
            acc = jnp.zeros((tr, tc), F32)
            for b in range(SUBLANES):
                xs_ref[b] = pad_ref[pl.ds(rb * tr + b, tr + span), pl.ds(c0, tc)]
                for a in range(n_a):
                    t = SUBLANES * a + b - off
                    if 0 <= t < CONV_WIDTH:
                        acc = acc + xs_ref[b, SUBLANES * a:SUBLANES * a + tr, :] * w[t:t + 1, :]
            y_ref[pl.ds(rb * tr, tr), pl.ds(c0, tc)] = acc
        return carry

    lax.fori_loop(0, D // tc, cols, 0)

    y = y_ref[...] + bdw_ref[...]
    mu = jnp.mean(y, axis=-1, keepdims=True)
    yc = y - mu
    var = jnp.mean(yc * yc, axis=-1, keepdims=True)
    z = yc * lax.rsqrt(var + LN_EPS) * lg_ref[...] + lb_ref[...]
    z = (z * jax.nn.sigmoid(z)).astype(BF16)
    o_ref[...] = x_ref[...] + jnp.dot(z, w2_ref[...], preferred_element_type=F32) + b2_ref[...]


def conv_residual(x, u, w_dw, b_dw, ln_g, ln_b, w_pw2, b_pw2, *, n_seq, tm=512, tc=128, tr=512):
    T, D = x.shape
    S = T // n_seq
    tiles_per_seq = S // tm
    hb = tm // HALO
    n_hb = T // HALO
    kern = functools.partial(_conv_kernel, tm=tm, tc=tc, tr=tr, tiles_per_seq=tiles_per_seq)
    tap_span = (HALO - CONV_PAD + CONV_WIDTH - 1) // SUBLANES * SUBLANES
    row = lambda v: v.reshape(1, D)
    return pl.pallas_call(
        kern,
        grid=(T // tm,),
        in_specs=[
            pl.BlockSpec((tm, D), lambda i: (i, 0)),
            pl.BlockSpec((tm, D), lambda i: (i, 0)),
            pl.BlockSpec((HALO, D), lambda i: (jnp.maximum(i * hb - 1, 0), 0)),
            pl.BlockSpec((HALO, D), lambda i: (jnp.minimum((i + 1) * hb, n_hb - 1), 0)),
            pl.BlockSpec((CONV_WIDTH, D), lambda i: (0, 0)),
            pl.BlockSpec((1, D), lambda i: (0, 0)),
            pl.BlockSpec((1, D), lambda i: (0, 0)),
            pl.BlockSpec((1, D), lambda i: (0, 0)),
            pl.BlockSpec((D, D), lambda i: (0, 0)),
            pl.BlockSpec((1, D), lambda i: (0, 0)),
        ],
        out_specs=pl.BlockSpec((tm, D), lambda i: (i, 0)),
        out_shape=jax.ShapeDtypeStruct((T, D), F32),
        scratch_shapes=[
            pltpu.VMEM((tm + 2 * HALO, D), F32),
            pltpu.VMEM((tm, D), F32),
            pltpu.VMEM((SUBLANES, tr + tap_span, tc), F32),
        ],
        compiler_params=_params(("parallel",)),
        name="conv",
    )(x, u, u, u, w_dw, row(b_dw), row(ln_g), row(ln_b), w_pw2, row(b_pw2))


def _lambda_init_for(layer_idx):
    return 0.8 - 0.6 * math.exp(-0.3 * layer_idx)


def kernel(x_prompt, x_sample, attn_norm_g, w_qkv, lam_q1, lam_k1, lam_q2, lam_k2, subln_g, w_o,
           conv_norm_g, conv_w_pw1, conv_b_pw1, conv_w_dw, conv_b_dw, conv_ln_g, conv_ln_b,
           conv_w_pw2, conv_b_pw2, mlp_norm_g, w_up, w_down, final_norm_g):
    D = D_MODEL
    n_prompt = x_prompt.shape[0]
    n_seq = n_prompt + x_sample.shape[0]
    xa = x_prompt.reshape(-1, D)
    xb = x_sample.reshape(-1, D)

    w_qkv_b = w_qkv[0].astype(BF16)
    lam_vecs = jnp.stack([lam_q1[0], lam_k1[0], lam_q2[0], lam_k2[0]]).astype(F32)
    qkv = norm_matmul(xa, xb, attn_norm_g[0], w_qkv_b, tm=1024, tn=1024)
    o = diff_attention(qkv, lam_vecs, subln_g[0], _lambda_init_for(0), n_seq=n_seq)
    x = proj_residual(xa, xb, o, w_o[0].astype(BF16), tm=512)
    w_up_b = w_up.astype(BF16)
    w_down_b = w_down.astype(BF16)
    x = mlp_residual(x, mlp_norm_g[0], w_up_b, w_down_b, final_norm_g,
                     layer=0, final_norm=False, tm=1024, tf=1024)

    u = norm_glu(x, conv_norm_g[0], conv_w_pw1[0].astype(BF16), conv_b_pw1[0], tm=1024, tn=1024)
    x = conv_residual(x, u, conv_w_dw[0], conv_b_dw[0], conv_ln_g[0], conv_ln_b[0],
                      conv_w_pw2[0].astype(BF16), conv_b_pw2[0], n_seq=n_seq)
    last = functools.partial(mlp_residual, x, mlp_norm_g[1], w_up_b, w_down_b, final_norm_g,
                             layer=1, final_norm=True, tm=1024, tf=1024)
    y_prompt = last(row_start=0, n_rows=xa.shape[0])
    y_sample = last(row_start=xa.shape[0], n_rows=xb.shape[0])
    return (y_prompt.reshape(x_prompt.shape), y_sample.reshape(x_sample.shape))
```

```python
import functools
import math

import jax
import jax.numpy as jnp
from jax import lax
from jax.experimental import pallas as pl
from jax.experimental.pallas import tpu as pltpu

D_MODEL = 2048
SEQ = 4096
HEAD_DIM = 64
N_HEADS = D_MODEL // (2 * HEAD_DIM)
V_DIM = 2 * HEAD_DIM
CONV_WIDTH = 31
CONV_PAD = (CONV_WIDTH - 1) // 2
D_FF = 4 * D_MODEL
RMS_EPS = 1e-6
SUBLN_EPS = 1e-5
LN_EPS = 1e-5

LANES = 128
SUBLANES = 8
VMEM_LIMIT = 56 * 1024 * 1024

F32 = jnp.float32
BF16 = jnp.bfloat16


ATTN_VMEM_LIMIT = 62 * 1024 * 1024
MLP_VMEM_LIMIT = 62 * 1024 * 1024
QKV_VMEM_LIMIT = 62 * 1024 * 1024


def _params(semantics, vmem_limit=VMEM_LIMIT):
    return pltpu.CompilerParams(dimension_semantics=semantics, vmem_limit_bytes=vmem_limit)


def _rmsnorm_rows(x, g, eps):
    return x * lax.rsqrt(jnp.mean(x * x, axis=-1, keepdims=True) + eps) * g


def _split_rows_specs(tm, D, n_a, rank):
    if rank == 1:
        return [pl.BlockSpec((tm, D), lambda i: (jnp.minimum(i, n_a - 1), 0)),
                pl.BlockSpec((tm, D), lambda i: (jnp.maximum(i - n_a, 0), 0))]
    return [pl.BlockSpec((tm, D), lambda i, j: (jnp.minimum(i, n_a - 1), 0)),
            pl.BlockSpec((tm, D), lambda i, j: (jnp.maximum(i - n_a, 0), 0))]


def _norm_matmul_kernel(xa_ref, xb_ref, g_ref, w_ref, o_ref, xn_ref, *, n_a):
    i = pl.program_id(0)
    first = pl.program_id(1) == 0

    @pl.when(first & (i < n_a))
    def _():
        xn_ref[...] = _rmsnorm_rows(xa_ref[...], g_ref[...], RMS_EPS).astype(BF16)

    @pl.when(first & (i >= n_a))
    def _():
        xn_ref[...] = _rmsnorm_rows(xb_ref[...], g_ref[...], RMS_EPS).astype(BF16)

    o_ref[...] = jnp.dot(xn_ref[...], w_ref[...], preferred_element_type=F32).astype(o_ref.dtype)


def norm_matmul(xa, xb, g, w, *, tm, tn):
    D = xa.shape[1]
    T = xa.shape[0] + xb.shape[0]
    N = w.shape[1]
    n_a = xa.shape[0] // tm
    return pl.pallas_call(
        functools.partial(_norm_matmul_kernel, n_a=n_a),
        grid=(T // tm, N // tn),
        in_specs=_split_rows_specs(tm, D, n_a, 2) + [
            pl.BlockSpec((1, D), lambda i, j: (0, 0)),
            pl.BlockSpec((D, tn), lambda i, j: (0, j)),
        ],
        out_specs=pl.BlockSpec((tm, tn), lambda i, j: (i, j)),
        out_shape=jax.ShapeDtypeStruct((T, N), BF16),
        scratch_shapes=[pltpu.VMEM((tm, D), BF16)],
        compiler_params=_params(("parallel", "arbitrary"), QKV_VMEM_LIMIT),
        name="qkv",
    )(xa, xb, g.reshape(1, D), w)


def _attn_kernel(lam_ref, g_ref, q0_ref, q1_ref, k0_ref, k1_ref, v_ref, o_ref,
                 bias_ref, vaug_ref, s0_ref, s1_ref, m0_ref, m1_ref, a0_ref, a1_ref,
                 *, tq, tb, nq, n_tiles, lambda_init):
    S = k0_ref.shape[0]
    g = pl.program_id(0)
    ga = jnp.minimum(g, n_tiles - 1)
    gb = jnp.clip(g - 1, 0, n_tiles - 1)
    qi = ga % nq
    h = (ga // nq) % N_HEADS

    @pl.when(g == 0)
    def _():
        s1_ref[...] = jnp.zeros_like(s1_ref)
        m1_ref[...] = jnp.zeros_like(m1_ref)
        a1_ref[...] = jnp.ones_like(a1_ref)
        r = lax.broadcasted_iota(jnp.int32, (tb, 2 * S), 0)
        j = lax.broadcasted_iota(jnp.int32, (tb, 2 * S), 1)
        bias_ref[...] = jnp.abs(r - j + S).astype(F32)

    neg_slope = -jnp.exp2(jnp.full((1, 1), -0.5, F32) * (h + 1).astype(F32))

    @pl.when(gb % nq == 0)
    def _():
        lane = lax.broadcasted_iota(jnp.int32, (S, LANES), 1)
        vaug_ref[:, :V_DIM] = v_ref[...]
        vaug_ref[:, V_DIM:] = jnp.where(lane == 0, 1.0, 0.0).astype(BF16)

    lv = lam_ref[...]
    lam = (jnp.exp(jnp.sum(lv[0:1] * lv[1:2], axis=-1, keepdims=True))
           - jnp.exp(jnp.sum(lv[2:3] * lv[3:4], axis=-1, keepdims=True)) + lambda_init)
    b0 = pl.multiple_of(S - qi * tq, LANES)

    def step(s_w, m_w, a_w, s_r, m_r, a_r):
        lane = lax.broadcasted_iota(jnp.int32, (tq, LANES), 1)
        mine = (lane >= HEAD_DIM).astype(jnp.int32) == h % 2
        scale = jnp.asarray(HEAD_DIM ** -0.5, BF16)
        zero = jnp.zeros((tq, LANES), BF16)
        qa = jnp.where(mine, q0_ref[...], zero) * scale
        qb = jnp.where(mine, q1_ref[...], zero) * scale
        qq = jnp.concatenate([jnp.concatenate([qa, zero], axis=1),
                              jnp.concatenate([zero, qb], axis=1)], axis=0)
        kk = jnp.concatenate([k0_ref[...], k1_ref[...]], axis=1)
        s = lax.dot_general(qq, kk, (((1,), (1,)), ((), ())), preferred_element_type=F32)
        bias = jnp.concatenate(
            [bias_ref[:, pl.ds(pl.multiple_of(b0 - r0, LANES), S)] for r0 in range(0, tq, tb)],
            axis=0) * neg_slope
        s = s + jnp.concatenate([bias, bias], axis=0)
        s_w[...] = s
        mp = s[:, :LANES]
        for c in range(1, S // LANES):
            mp = jnp.maximum(mp, s[:, c * LANES:(c + 1) * LANES])
        m_w[...] = mp

        m = jnp.max(m_r[...], axis=-1, keepdims=True)
        p = jnp.exp(s_r[...] - m).astype(BF16)
        a_w[...] = jnp.dot(p, vaug_ref[...], preferred_element_type=F32)

        acc = a_r[...]
        o0 = acc[:tq, :V_DIM] / acc[:tq, V_DIM:V_DIM + 1]
        o1 = acc[tq:, :V_DIM] / acc[tq:, V_DIM:V_DIM + 1]
        o = o0 - lam * o1
        o = _rmsnorm_rows(o, g_ref[...], SUBLN_EPS) * (1.0 - lambda_init)
        o_ref[...] = o.astype(o_ref.dtype)

    @pl.when(g % 2 == 0)
    def _():
        step(s0_ref, m0_ref, a0_ref, s1_ref, m1_ref, a1_ref)

    @pl.when(g % 2 == 1)
    def _():
        step(s1_ref, m1_ref, a1_ref, s0_ref, m0_ref, a0_ref)


def diff_attention(qkv, lam_vecs, subln_g, lambda_init, *, n_seq, tq=512, tb=256):
    T = qkv.shape[0]
    S = T // n_seq
    nq = S // tq
    H = N_HEADS
    n_tiles = n_seq * H * nq
    kern = functools.partial(_attn_kernel, tq=tq, tb=tb, nq=nq, n_tiles=n_tiles, lambda_init=lambda_init)

    def tile(t):
        return t // (nq * H), (t // nq) % H, t % nq

    def q_map(m):
        def index(g):
            b, h, i = tile(jnp.minimum(g, n_tiles - 1))
            return (b * nq + i, m * (H // 2) + h // 2)
        return index

    def k_map(m):
        def index(g):
            b, h, _ = tile(jnp.minimum(g, n_tiles - 1))
            return (b, H + m * (H // 2) + h // 2)
        return index

    def v_map(g):
        b, h, _ = tile(jnp.clip(g - 1, 0, n_tiles - 1))
        return (b, 2 * H + h)

    def o_map(g):
        b, h, i = tile(jnp.maximum(g - 2, 0))
        return (b * nq + i, h)

    return pl.pallas_call(
        kern,
        grid=(n_tiles + 2,),
        in_specs=[
            pl.BlockSpec((4, HEAD_DIM), lambda g: (0, 0)),
            pl.BlockSpec((1, V_DIM), lambda g: (0, 0)),
            pl.BlockSpec((tq, LANES), q_map(0)),
            pl.BlockSpec((tq, LANES), q_map(1)),
            pl.BlockSpec((S, LANES), k_map(0)),
            pl.BlockSpec((S, LANES), k_map(1)),
            pl.BlockSpec((S, LANES), v_map),
        ],
        out_specs=pl.BlockSpec((tq, V_DIM), o_map),
        out_shape=jax.ShapeDtypeStruct((T, D_MODEL), BF16),
        scratch_shapes=[
            pltpu.VMEM((tb, 2 * S), F32),
            pltpu.VMEM((S, 2 * LANES), BF16),
            pltpu.VMEM((2 * tq, S), F32),
            pltpu.VMEM((2 * tq, S), F32),
            pltpu.VMEM((2 * tq, LANES), F32),
            pltpu.VMEM((2 * tq, LANES), F32),
            pltpu.VMEM((2 * tq, 2 * LANES), F32),
            pltpu.VMEM((2 * tq, 2 * LANES), F32),
        ],
        compiler_params=_params(("arbitrary",), ATTN_VMEM_LIMIT),
        name="attn",
    )(lam_vecs, subln_g.reshape(1, V_DIM), qkv, qkv, qkv, qkv, qkv)


def _proj_kernel(xa_ref, xb_ref, a_ref, w_ref, o_ref, *, n_a):
    x = jnp.where(pl.program_id(0) < n_a, xa_ref[...], xb_ref[...])
    o_ref[...] = x + jnp.dot(a_ref[...], w_ref[...], preferred_element_type=F32)


def proj_residual(xa, xb, a, w, *, tm):
    T, D = a.shape
    n_a = xa.shape[0] // tm
    return pl.pallas_call(
        functools.partial(_proj_kernel, n_a=n_a),
        grid=(T // tm,),
        in_specs=_split_rows_specs(tm, D, n_a, 1) + [
            pl.BlockSpec((tm, D), lambda i: (i, 0)),
            pl.BlockSpec((D, D), lambda i: (0, 0)),
        ],
        out_specs=pl.BlockSpec((tm, D), lambda i: (i, 0)),
        out_shape=jax.ShapeDtypeStruct((T, D), F32),
        compiler_params=_params(("parallel",)),
        name="proj",
    )(xa, xb, a, w)


def _mlp_kernel(x_ref, g_ref, wu_ref, wd_ref, gf_ref, o_ref, xn_ref, *, final_norm):
    j = pl.program_id(1)

    @pl.when(j == 0)
    def _():
        x = x_ref[...]
        xn_ref[...] = _rmsnorm_rows(x, g_ref[...], RMS_EPS).astype(BF16)
        o_ref[...] = x

    hdn = jnp.dot(xn_ref[...], wu_ref[...], preferred_element_type=F32)
    hdn = jnp.maximum(hdn, 0.0)
    hdn = (hdn * hdn).astype(BF16)
    o_ref[...] += jnp.dot(hdn, wd_ref[...], preferred_element_type=F32)

    if final_norm:
        @pl.when(j == pl.num_programs(1) - 1)
        def _():
            o_ref[...] = _rmsnorm_rows(o_ref[...], gf_ref[...], RMS_EPS)


def mlp_residual(x, g, w_up, w_down, g_final, *, layer, final_norm, tm, tf, row_start=0, n_rows=None):
    D = x.shape[1]
    n_rows = x.shape[0] if n_rows is None else n_rows
    F = w_up.shape[2]
    i0 = row_start // tm
    kern = functools.partial(_mlp_kernel, final_norm=final_norm)
    return pl.pallas_call(
        kern,
        grid=(n_rows // tm, F // tf),
        in_specs=[
            pl.BlockSpec((tm, D), lambda i, j: (i0 + i, 0)),
            pl.BlockSpec((1, D), lambda i, j: (0, 0)),
            pl.BlockSpec((None, D, tf), lambda i, j: (layer, 0, j)),
            pl.BlockSpec((None, tf, D), lambda i, j: (layer, j, 0)),
            pl.BlockSpec((1, D), lambda i, j: (0, 0)),
        ],
        out_specs=pl.BlockSpec((tm, D), lambda i, j: (i, 0)),
        out_shape=jax.ShapeDtypeStruct((n_rows, D), F32),
        scratch_shapes=[pltpu.VMEM((tm, D), BF16)],
        compiler_params=_params(("parallel", "arbitrary"), MLP_VMEM_LIMIT),
        name="mlp",
    )(x, g.reshape(1, D), w_up, w_down, g_final.reshape(1, D))


def _glu_kernel(x_ref, g_ref, wa_ref, wg_ref, ba_ref, bg_ref, o_ref, xn_ref):
    @pl.when(pl.program_id(1) == 0)
    def _():
        xn_ref[...] = _rmsnorm_rows(x_ref[...], g_ref[...], RMS_EPS).astype(BF16)

    xn = xn_ref[...]
    a = jnp.dot(xn, wa_ref[...], preferred_element_type=F32) + ba_ref[...]
    gt = jnp.dot(xn, wg_ref[...], preferred_element_type=F32) + bg_ref[...]
    o_ref[...] = a * jax.nn.sigmoid(gt)


def norm_glu(x, g, w_pw1, b_pw1, *, tm, tn):
    T, D = x.shape
    nj = D // tn
    b2 = b_pw1.reshape(1, 2 * D)
    return pl.pallas_call(
        _glu_kernel,
        grid=(T // tm, nj),
        in_specs=[
            pl.BlockSpec((tm, D), lambda i, j: (i, 0)),
            pl.BlockSpec((1, D), lambda i, j: (0, 0)),
            pl.BlockSpec((D, tn), lambda i, j: (0, j)),
            pl.BlockSpec((D, tn), lambda i, j: (0, nj + j)),
            pl.BlockSpec((1, tn), lambda i, j: (0, j)),
            pl.BlockSpec((1, tn), lambda i, j: (0, nj + j)),
        ],
        out_specs=pl.BlockSpec((tm, tn), lambda i, j: (i, j)),
        out_shape=jax.ShapeDtypeStruct((T, D), F32),
        scratch_shapes=[pltpu.VMEM((tm, D), BF16)],
        compiler_params=_params(("parallel", "arbitrary")),
        name="glu",
    )(x, g.reshape(1, D), w_pw1, w_pw1, b2, b2)


HALO = 16


def _conv_kernel(x_ref, u_ref, up_ref, un_ref, wdw_ref, bdw_ref, lg_ref, lb_ref, w2_ref, b2_ref,
                 o_ref, pad_ref, y_ref, xs_ref, *, tm, tc, tr, tiles_per_seq):
    i = pl.program_id(0)
    first = (i % tiles_per_seq) == 0
    last = (i % tiles_per_seq) == tiles_per_seq - 1
    pad_ref[0:HALO, :] = jnp.where(first, 0.0, up_ref[...])
    pad_ref[HALO:HALO + tm, :] = u_ref[...]
    pad_ref[HALO + tm:, :] = jnp.where(last, 0.0, un_ref[...])

    D = u_ref.shape[1]
    off = HALO - CONV_PAD

    span = xs_ref.shape[1] - tr
    n_a = span // SUBLANES + 1

    def cols(c, carry):
        c0 = pl.multiple_of(c * tc, tc)
        w = wdw_ref[:, pl.ds(c0, tc)]
        for rb in range(tm // tr):
            acc = jnp.zeros((tr, tc), F32)
            for b in range(SUBLANES):
                xs_ref[b] = pad_ref[pl.ds(rb * tr + b, tr + span), pl.ds(c0, tc)]
                for a in range(n_a):
                    t = SUBLANES * a + b - off
                    if 0 <= t < CONV_WIDTH:
                        acc = acc + xs_ref[b, SUBLANES * a:SUBLANES * a + tr, :] * w[t:t + 1, :]
            y_ref[pl.ds(rb * tr, tr), pl.ds(c0, tc)] = acc
        return carry

    lax.fori_loop(0, D // tc, cols, 0)

    y = y_ref[...] + bdw_ref[...]
    mu = jnp.mean(y, axis=-1, keepdims=True)
    yc = y - mu
    var = jnp.mean(yc * yc, axis=-1, keepdims=True)
    z = yc * lax.rsqrt(var + LN_EPS) * lg_ref[...] + lb_ref[...]
    z = (z * jax.nn.sigmoid(z)).astype(BF16)
    o_ref[...] = x_ref[...] + jnp.dot(z, w2_ref[...], preferred_element_type=F32) + b2_ref[...]


def conv_residual(x, u, w_dw, b_dw, ln_g, ln_b, w_pw2, b_pw2, *, n_seq, tm=512, tc=128, tr=512):
    T, D = x.shape
    S = T // n_seq
    tiles_per_seq = S // tm
    hb = tm // HALO
    n_hb = T // HALO
    kern = functools.partial(_conv_kernel, tm=tm, tc=tc, tr=tr, tiles_per_seq=tiles_per_seq)
    tap_span = (HALO - CONV_PAD + CONV_WIDTH - 1) // SUBLANES * SUBLANES
    row = lambda v: v.reshape(1, D)
    return pl.pallas_call(
        kern,
        grid=(T // tm,),
        in_specs=[
            pl.BlockSpec((tm, D), lambda i: (i, 0)),
            pl.BlockSpec((tm, D), lambda i: (i, 0)),
            pl.BlockSpec((HALO, D), lambda i: (jnp.maximum(i * hb - 1, 0), 0)),
            pl.BlockSpec((HALO, D), lambda i: (jnp.minimum((i + 1) * hb, n_hb - 1), 0)),
            pl.BlockSpec((CONV_WIDTH, D), lambda i: (0, 0)),
            pl.BlockSpec((1, D), lambda i: (0, 0)),
            pl.BlockSpec((1, D), lambda i: (0, 0)),
            pl.BlockSpec((1, D), lambda i: (0, 0)),
            pl.BlockSpec((D, D), lambda i: (0, 0)),
            pl.BlockSpec((1, D), lambda i: (0, 0)),
        ],
        out_specs=pl.BlockSpec((tm, D), lambda i: (i, 0)),
        out_shape=jax.ShapeDtypeStruct((T, D), F32),
        scratch_shapes=[
            pltpu.VMEM((tm + 2 * HALO, D), F32),
            pltpu.VMEM((tm, D), F32),
            pltpu.VMEM((SUBLANES, tr + tap_span, tc), F32),
        ],
        compiler_params=_params(("parallel",)),
        name="conv",
    )(x, u, u, u, w_dw, row(b_dw), row(ln_g), row(ln_b), w_pw2, row(b_pw2))


def _lambda_init_for(layer_idx):
    return 0.8 - 0.6 * math.exp(-0.3 * layer_idx)


def kernel(x_prompt, x_sample, attn_norm_g, w_qkv, lam_q1, lam_k1, lam_q2, lam_k2, subln_g, w_o,
           conv_norm_g, conv_w_pw1, conv_b_pw1, conv_w_dw, conv_b_dw, conv_ln_g, conv_ln_b,
           conv_w_pw2, conv_b_pw2, mlp_norm_g, w_up, w_down, final_norm_g):
    D = D_MODEL
    n_prompt = x_prompt.shape[0]
    n_seq = n_prompt + x_sample.shape[0]
    xa = x_prompt.reshape(-1, D)
    xb = x_sample.reshape(-1, D)

    w_qkv_b = w_qkv[0].astype(BF16)
    lam_vecs = jnp.stack([lam_q1[0], lam_k1[0], lam_q2[0], lam_k2[0]]).astype(F32)
    qkv = norm_matmul(xa, xb, attn_norm_g[0], w_qkv_b, tm=1024, tn=1536)
    o = diff_attention(qkv, lam_vecs, subln_g[0], _lambda_init_for(0), n_seq=n_seq)
    x = proj_residual(xa, xb, o, w_o[0].astype(BF16), tm=512)
    w_up_b = w_up.astype(BF16)
    w_down_b = w_down.astype(BF16)
    x = mlp_residual(x, mlp_norm_g[0], w_up_b, w_down_b, final_norm_g,
                     layer=0, final_norm=False, tm=1024, tf=1024)

    u = norm_glu(x, conv_norm_g[0], conv_w_pw1[0].astype(BF16), conv_b_pw1[0], tm=1024, tn=1024)
    x = conv_residual(x, u, conv_w_dw[0], conv_b_dw[0], conv_ln_g[0], conv_ln_b[0],
                      conv_w_pw2[0].astype(BF16), conv_b_pw2[0], n_seq=n_seq)
    last = functools.partial(mlp_residual, x, mlp_norm_g[1], w_up_b, w_down_b, final_norm_g,
                             layer=1, final_norm=True, tm=1024, tf=1024)
    y_prompt = last(row_start=0, n_rows=xa.shape[0])
    y_sample = last(row_start=xa.shape[0], n_rows=xb.shape[0])
    return (y_prompt.reshape(x_prompt.shape), y_sample.reshape(x_sample.shape))
```
